```python
import jax, jax.numpy as jnp
from jax import lax
import numpy as np

D_MODEL = 1024
BATCH = 32
SEQ = 2048
DEPTH = 1

MEM_LEN = 256
D_MIX = D_MODEL
SB_HEAD_DIM = 64
SB_WIDTH = D_MIX // 2
SB_HEADS = SB_WIDTH // SB_HEAD_DIM
POOL_WIDTH = D_MIX - SB_WIDTH
POOL_WINDOWS = (2, 4, 8, 16)
POOL_GROUPS = len(POOL_WINDOWS)
POOL_GROUP_DIM = POOL_WIDTH // POOL_GROUPS
IN_COLS = 3 * SB_WIDTH + POOL_WIDTH
Q_BLOCK = 128
MEM_HEADS = 4
MEM_HEAD_DIM = D_MODEL // MEM_HEADS
D_FF = 256 * ((8 * D_MODEL // 3 + 255) // 256)
FFN_RESIDUAL_WEIGHT = 0.5
EPS = 1e-6

kernel_name = "hybrid_stickbreaking_pool_macaron_layer"


def rmsnorm(x, g):
    xf = x.astype(jnp.float32)
    xf = xf * lax.rsqrt(jnp.mean(xf * xf, axis=-1, keepdims=True) + EPS)
    return xf.astype(x.dtype) * g


def swiglu(h, w_gate, w_up, w_down):
    return (jax.nn.silu(h @ w_gate) * (h @ w_up)) @ w_down


def stick_breaking_attention(q, k, v):
    S = q.shape[2]
    scale = q.shape[-1] ** -0.5
    outs = []
    for i in range(S // Q_BLOCK):
        q0 = i * Q_BLOCK
        kv_len = q0 + Q_BLOCK
        q_blk = q[:, :, q0:kv_len]
        k_pre = k[:, :, :kv_len]
        v_pre = v[:, :, :kv_len]
        z = jnp.einsum("bhqd,bhkd->bhqk", q_blk, k_pre).astype(jnp.float32) * scale
        q_pos = q0 + jnp.arange(Q_BLOCK)
        k_pos = jnp.arange(kv_len)
        strict = k_pos[None, :] < q_pos[:, None]
        log_rest = jnp.where(strict, jax.nn.log_sigmoid(-z), 0.0)
        suffix = lax.cumsum(log_rest, axis=3, reverse=True) - log_rest
        a = jnp.where(strict, jnp.exp(jax.nn.log_sigmoid(z) + suffix), 0.0)
        outs.append(jnp.einsum("bhqk,bhkd->bhqd", a.astype(v.dtype), v_pre))
    return jnp.concatenate(outs, axis=2)


def causal_multiscale_pool(u):
    B, S, _ = u.shape
    uf = u.astype(jnp.float32).reshape(B, S, POOL_GROUPS, POOL_GROUP_DIM)
    cs = jnp.concatenate([jnp.zeros_like(uf[:, :1]), jnp.cumsum(uf, axis=1)], axis=1)
    pos = jnp.arange(S)
    pooled = []
    for g, w in enumerate(POOL_WINDOWS):
        hi = cs[:, 1:, g]
        lo = cs[:, jnp.maximum(pos + 1 - w, 0), g]
        count = jnp.minimum(pos + 1, w).astype(jnp.float32)
        pooled.append((hi - lo) / count[None, :, None])
    pooled = jnp.stack(pooled, axis=2)
    return (pooled - uf).astype(u.dtype)


def memory_cross_attention(h, mem_n, w_q, w_kv, w_o):
    B, S, _ = h.shape
    M = mem_n.shape[1]
    q = (h @ w_q).reshape(B, S, MEM_HEADS, MEM_HEAD_DIM)
    kv = (mem_n @ w_kv).reshape(B, M, 2, MEM_HEADS, MEM_HEAD_DIM)
    k, v = kv[:, :, 0], kv[:, :, 1]
    s = jnp.einsum("bshd,bmhd->bhsm", q, k).astype(jnp.float32) * (MEM_HEAD_DIM ** -0.5)
    p = jax.nn.softmax(s, axis=-1).astype(v.dtype)
    o = jnp.einsum("bhsm,bmhd->bshd", p, v).reshape(B, S, D_MODEL)
    return o @ w_o


def parallel_head_group_mixer(h, w_in, w_pool, pool_scale, w_out):
    B, S, _ = h.shape
    proj = h @ w_in
    q, k, v, u = jnp.split(proj, [SB_WIDTH, 2 * SB_WIDTH, 3 * SB_WIDTH], axis=-1)
    to_heads = lambda t: t.reshape(B, S, SB_HEADS, SB_HEAD_DIM).transpose(0, 2, 1, 3)
    o_sb = stick_breaking_attention(to_heads(q), to_heads(k), to_heads(v))
    o_sb = o_sb.transpose(0, 2, 1, 3).reshape(B, S, SB_WIDTH)
    pooled = causal_multiscale_pool(u)
    o_pool = jnp.einsum("bsgc,gcd->bsgd", pooled, w_pool).reshape(B, S, POOL_WIDTH) * pool_scale
    return jnp.concatenate([o_sb, o_pool], axis=-1) @ w_out


def setup_inputs(seed: int = 0) -> dict:
    key = jax.random.key(seed)
    ks = jax.random.split(key, 24)

    def dense(k, shape, fan_in):
        return jax.random.normal(k, shape, jnp.float32) * fan_in ** -0.5

    def gain(k, shape):
        return 1.0 + 0.02 * jax.random.normal(k, shape, jnp.float32)

    L, D = DEPTH, D_MODEL
    return {
        "x": jax.random.normal(ks[0], (BATCH, SEQ, D), jnp.float32),
        "mem": jax.random.normal(ks[1], (BATCH, MEM_LEN, D), jnp.float32),
        "ffn1_norm": gain(ks[2], (L, D)),
        "ffn1_w_gate": dense(ks[3], (L, D, D_FF), D),
        "ffn1_w_up": dense(ks[4], (L, D, D_FF), D),
        "ffn1_w_down": dense(ks[5], (L, D_FF, D), D_FF),
        "mix_norm": gain(ks[6], (L, D)),
        "w_in": dense(ks[7], (L, D, IN_COLS), D),
        "w_pool": dense(ks[8], (L, POOL_GROUPS, POOL_GROUP_DIM, POOL_GROUP_DIM), POOL_GROUP_DIM),
        "pool_scale": gain(ks[9], (L, POOL_WIDTH)),
        "w_out": dense(ks[10], (L, D_MIX, D), D_MIX),
        "mem_q_norm": gain(ks[11], (L, D)),
        "mem_kv_norm": gain(ks[12], (L, D)),
        "mem_w_q": dense(ks[13], (L, D, D), D),
        "mem_w_kv": dense(ks[14], (L, D, 2 * D), D),
        "mem_w_o": dense(ks[15], (L, D, D), D),
        "ffn2_norm": gain(ks[16], (L, D)),
        "ffn2_w_gate": dense(ks[17], (L, D, D_FF), D),
        "ffn2_w_up": dense(ks[18], (L, D, D_FF), D),
        "ffn2_w_down": dense(ks[19], (L, D_FF, D), D_FF),
        "final_norm": gain(ks[20], (D,)),
    }


def reference(x, mem, ffn1_norm, ffn1_w_gate, ffn1_w_up, ffn1_w_down, mix_norm, w_in, w_pool,
              pool_scale, w_out, mem_q_norm, mem_kv_norm, mem_w_q, mem_w_kv, mem_w_o,
              ffn2_norm, ffn2_w_gate, ffn2_w_up, ffn2_w_down, final_norm):
    for l in range(DEPTH):
        x = x + FFN_RESIDUAL_WEIGHT * swiglu(rmsnorm(x, ffn1_norm[l]), ffn1_w_gate[l], ffn1_w_up[l], ffn1_w_down[l])
        x = x + parallel_head_group_mixer(rmsnorm(x, mix_norm[l]), w_in[l], w_pool[l], pool_scale[l], w_out[l])
        x = x + memory_cross_attention(rmsnorm(x, mem_q_norm[l]), rmsnorm(mem, mem_kv_norm[l]),
                                       mem_w_q[l], mem_w_kv[l], mem_w_o[l])
        x = x + FFN_RESIDUAL_WEIGHT * swiglu(rmsnorm(x, ffn2_norm[l]), ffn2_w_gate[l], ffn2_w_up[l], ffn2_w_down[l])
    return rmsnorm(x, final_norm)
```

```python
import functools

import jax
import jax.numpy as jnp
from jax import lax
from jax.experimental import pallas as pl
from jax.experimental.pallas import tpu as pltpu

F32 = jnp.float32
BF16 = jnp.bfloat16

EPS = 1e-6
FFN_RESIDUAL_WEIGHT = 0.5
SB_HEAD_DIM = 64
SB_HEADS_PER_STEP = 2
POOL_WINDOWS = (2, 4, 8, 16)
POOL_HALO = 16
MEM_HEADS = 4

LANES = 128
SB_TQ = 256
SB_TK = 256
VMEM_LIMIT = 60 * 1024 * 1024


def _rms(x, g):
    return x * lax.rsqrt(jnp.mean(x * x, axis=-1, keepdims=True) + EPS) * g


def _resident(shape):
    zeros = (0,) * len(shape)
    return pl.BlockSpec(shape, lambda *_: zeros, pipeline_mode=pl.Buffered(1))


def _params(*sem):
    return pltpu.CompilerParams(dimension_semantics=sem, vmem_limit_bytes=VMEM_LIMIT)


def _ffn_kernel(x_ref, g_ref, wg_ref, wu_ref, wd_ref, *rest, final_norm):
    o_ref = rest[-1]
    x = x_ref[...]
    h = _rms(x, g_ref[...]).astype(BF16)
    gate = jnp.dot(h, wg_ref[...], preferred_element_type=F32)
    up = jnp.dot(h, wu_ref[...], preferred_element_type=F32)
    act = (jax.nn.silu(gate) * up).astype(BF16)
    y = x + FFN_RESIDUAL_WEIGHT * jnp.dot(act, wd_ref[...], preferred_element_type=F32)
    if final_norm:
        y = _rms(y, rest[0][...])
    o_ref[...] = y


def _ffn(x2d, g, wg, wu, wd, gf=None, *, tm=512):
    t, d = x2d.shape
    dff = wg.shape[1]
    row = pl.BlockSpec((tm, d), lambda i: (i, 0))
    vec = _resident((1, d))
    in_specs = [row, vec, _resident((d, dff)), _resident((d, dff)), _resident((dff, d))]
    args = [x2d, g, wg, wu, wd]
    if gf is not None:
        in_specs.append(vec)
        args.append(gf)
    return pl.pallas_call(
        functools.partial(_ffn_kernel, final_norm=gf is not None),
        grid=(t // tm,),
        in_specs=in_specs,
        out_specs=row,
        out_shape=jax.ShapeDtypeStruct((t, d), F32),
        compiler_params=_params("parallel"),
        name="ffn_final" if gf is not None else "ffn",
    )(*args)


def _proj_kernel(x_ref, g_ref, w_ref, q_ref, k_ref, v_ref, u_ref, *, q_scale):
    h = _rms(x_ref[...], g_ref[...]).astype(BF16)
    p = jnp.dot(h, w_ref[...], preferred_element_type=F32)
    w = q_ref.shape[1]
    q_ref[...] = (p[:, :w] * q_scale).astype(BF16)
    k_ref[...] = p[:, w:2 * w].astype(BF16)
    v_ref[...] = p[:, 2 * w:3 * w].astype(BF16)
    u_ref[...] = p[:, 3 * w:]


def _proj(x2d, g, w_in, *, sb_width, tm=512):
    t, d = x2d.shape
    cols = w_in.shape[1]
    pool_width = cols - 3 * sb_width
    row = lambda n: pl.BlockSpec((tm, n), lambda i: (i, 0))
    return pl.pallas_call(
        functools.partial(_proj_kernel, q_scale=SB_HEAD_DIM ** -0.5),
        grid=(t // tm,),
        in_specs=[row(d), _resident((1, d)), _resident((d, cols))],
        out_specs=[row(sb_width), row(sb_width), row(sb_width), row(pool_width)],
        out_shape=[jax.ShapeDtypeStruct((t, sb_width), BF16)] * 3
        + [jax.ShapeDtypeStruct((t, pool_width), F32)],
        compiler_params=_params("parallel"),
        name="proj",
    )(x2d, g, w_in)


def _sb_tile(qm, kb, vb, tri, carry, acc, strict):
    z = lax.dot_general(qm, kb, (((1,), (1,)), ((), ())), preferred_element_type=F32)
    log_rest = -(jnp.maximum(z, 0.0) + jnp.log(1.0 + jnp.exp(-jnp.abs(z))))
    log_beta = z + log_rest
    if strict is not None:
        log_rest = jnp.where(strict, log_rest, 0.0)
    suffix = jnp.dot(log_rest.astype(BF16), tri, preferred_element_type=F32) + carry
    a = jnp.exp(log_beta + suffix)
    if strict is not None:
        a = jnp.where(strict, a, 0.0)
    acc = acc + jnp.dot(a.astype(BF16), vb, preferred_element_type=F32)
    carry = carry + jnp.sum(log_rest, axis=1, keepdims=True)
    return carry, acc


def _sb_kernel(q_ref, k_ref, v_ref, tri_ref, o_ref):
    seq = q_ref.shape[1]
    lane = lax.broadcasted_iota(jnp.int32, (1, LANES), 1)
    head_lanes = [lane < SB_HEAD_DIM, lane >= SB_HEAD_DIM]
    row = lax.broadcasted_iota(jnp.int32, (SB_TQ, SB_TK), 0)
    col = lax.broadcasted_iota(jnp.int32, (SB_TQ, SB_TK), 1)
    strict = col < row

    def q_block(qi, _):
        q0 = pl.multiple_of(qi * SB_TQ, SB_TQ)
        q = q_ref[0, pl.ds(q0, SB_TQ), :]
        qms = [jnp.where(m, q, jnp.zeros_like(q)) for m in head_lanes]
        tri = tri_ref[...]

        def tiles(k0, state, mask):
            kb = k_ref[0, pl.ds(k0, SB_TK), :]
            vb = v_ref[0, pl.ds(k0, SB_TK), :]
            return tuple(_sb_tile(qm, kb, vb, tri, c, acc, mask)
                         for qm, (c, acc) in zip(qms, state))

        zero = (jnp.zeros((SB_TQ, 1), F32), jnp.zeros((SB_TQ, LANES), F32))
        state = tiles(q0, (zero,) * SB_HEADS_PER_STEP, strict)

        def k_block(t, state):
            k0 = pl.multiple_of((qi - 1 - t) * SB_TK, SB_TK)
            return tiles(k0, state, None)

        state = lax.fori_loop(0, qi, k_block, state)
        (_, acc0), (_, acc1) = state
        o_ref[0, pl.ds(q0, SB_TQ), :] = jnp.where(head_lanes[0], acc0, acc1).astype(o_ref.dtype)
        return 0

    lax.fori_loop(0, seq // SB_TQ, q_block, 0)


def _sb_attention(q, k, v):
    b, s, w = q.shape
    assert SB_TQ == SB_TK and s % SB_TQ == 0 and w % LANES == 0
    assert SB_HEADS_PER_STEP * SB_HEAD_DIM == LANES
    idx = lax.broadcasted_iota(jnp.int32, (SB_TK, SB_TK), 0)
    tri = (idx > idx.T).astype(BF16)
    blk = pl.BlockSpec((1, s, LANES), lambda i, j: (i, 0, j))
    return pl.pallas_call(
        _sb_kernel,
        grid=(b, w // LANES),
        in_specs=[blk, blk, blk, _resident((SB_TK, SB_TK))],
        out_specs=blk,
        out_shape=jax.ShapeDtypeStruct((b, s, w), BF16),
        compiler_params=_params("parallel", "parallel"),
        name="sb_attn",
    )(q, k, v, tri)


def _mixout_kernel(x_ref, osb_ref, u_ref, wout_ref, wpool_ref, ps_ref, o_ref, ubuf):
    ts = u_ref.shape[1]
    sb_width = osb_ref.shape[2]
    gdim = wpool_ref.shape[1]
    si = pl.program_id(1)

    @pl.when(si == 0)
    def _():
        ubuf[0:POOL_HALO, :] = jnp.zeros((POOL_HALO, ubuf.shape[1]), F32)

    ubuf[POOL_HALO:, :] = u_ref[0]
    pos = si * ts + lax.broadcasted_iota(jnp.int32, (ts, 1), 0)
    pooled_out = []
    for g, w in enumerate(POOL_WINDOWS):
        cols = slice(g * gdim, (g + 1) * gdim)
        token = ubuf[POOL_HALO:, cols]
        total = token
        for d in range(1, w):
            total = total + ubuf[POOL_HALO - d:POOL_HALO - d + ts, cols]
        inv_count = 1.0 / jnp.minimum(pos + 1, w).astype(F32)
        pooled = (total * inv_count - token).astype(BF16)
        pooled_out.append(jnp.dot(pooled, wpool_ref[g], preferred_element_type=F32))
    o_pool = (jnp.concatenate(pooled_out, axis=1) * ps_ref[...]).astype(BF16)
    ubuf[0:POOL_HALO, :] = ubuf[ts:ts + POOL_HALO, :]
    y = x_ref[0] + jnp.dot(osb_ref[0], wout_ref[0:sb_width, :], preferred_element_type=F32)
    y = y + jnp.dot(o_pool, wout_ref[sb_width:, :], preferred_element_type=F32)
    o_ref[0] = y


def _mix_out(x, o_sb, u, w_out, w_pool, pool_scale, *, ts=512):
    b, s, d = x.shape
    sb_width = o_sb.shape[2]
    pool_width = u.shape[2]
    assert POOL_HALO >= max(POOL_WINDOWS) - 1 and ts >= POOL_HALO
    blk = lambda n: pl.BlockSpec((1, ts, n), lambda i, j: (i, j, 0))
    return pl.pallas_call(
        _mixout_kernel,
        grid=(b, s // ts),
        in_specs=[blk(d), blk(sb_width), blk(pool_width), _resident(w_out.shape),
                  _resident(w_pool.shape), _resident((1, pool_width))],
        out_specs=blk(d),
        out_shape=jax.ShapeDtypeStruct((b, s, d), F32),
        scratch_shapes=[pltpu.VMEM((ts + POOL_HALO, pool_width), F32)],
        compiler_params=_params("parallel", "arbitrary"),
        name="mix_out",
    )(x, o_sb, u, w_out, w_pool, pool_scale)


def _memkv_kernel(m_ref, g_ref, w_ref, o_ref):
    h = _rms(m_ref[...], g_ref[...]).astype(BF16)
    o_ref[...] = jnp.dot(h, w_ref[...], preferred_element_type=F32).astype(o_ref.dtype)


def _mem_kv(mem2d, g, w_kv, *, tm=1024):
    t, d = mem2d.shape
    n = w_kv.shape[1]
    return pl.pallas_call(
        _memkv_kernel,
        grid=(t // tm,),
        in_specs=[pl.BlockSpec((tm, d), lambda i: (i, 0)), _resident((1, d)), _resident((d, n))],
        out_specs=pl.BlockSpec((tm, n), lambda i: (i, 0)),
        out_shape=jax.ShapeDtypeStruct((t, n), BF16),
        compiler_params=_params("parallel"),
        name="mem_kv",
    )(mem2d, g, w_kv)


def _xattn_kernel(x_ref, g_ref, wq_ref, kv_ref, wo_ref, o_ref):
    x = x_ref[0]
    d = x.shape[1]
    hd = d // MEM_HEADS
    h = _rms(x, g_ref[...]).astype(BF16)
    q = (jnp.dot(h, wq_ref[...], preferred_element_type=F32) * hd ** -0.5).astype(BF16)
    heads = []
    for i in range(MEM_HEADS):
        qh = q[:, i * hd:(i + 1) * hd]
        kh = kv_ref[0, :, i * hd:(i + 1) * hd]
        vh = kv_ref[0, :, d + i * hd:d + (i + 1) * hd]
        s = lax.dot_general(qh, kh, (((1,), (1,)), ((), ())), preferred_element_type=F32)
        p = jnp.exp(s - jnp.max(s, axis=-1, keepdims=True))
        inv = 1.0 / jnp.sum(p, axis=-1, keepdims=True)
        heads.append((jnp.dot(p.astype(BF16), vh, preferred_element_type=F32) * inv).astype(BF16))
    o = jnp.concatenate(heads, axis=1)
    o_ref[0] = x + jnp.dot(o, wo_ref[...], preferred_element_type=F32)


def _xattn(x, g, w_q, kv, w_o, *, ts=512):
    b, s, d = x.shape
    m = kv.shape[1]
    blk = pl.BlockSpec((1, ts, d), lambda i, j: (i, j, 0))
    return pl.pallas_call(
        _xattn_kernel,
        grid=(b, s // ts),
        in_specs=[blk, _resident((1, d)), _resident((d, d)),
                  pl.BlockSpec((1, m, 2 * d), lambda i, j: (i, 0, 0)), _resident((d, d))],
        out_specs=blk,
        out_shape=jax.ShapeDtypeStruct((b, s, d), F32),
        compiler_params=_params("parallel", "parallel"),
        name="xattn",
    )(x, g, w_q, kv, w_o)


def kernel(x, mem, ffn1_norm, ffn1_w_gate, ffn1_w_up, ffn1_w_down, mix_norm, w_in, w_pool,
           pool_scale, w_out, mem_q_norm, mem_kv_norm, mem_w_q, mem_w_kv, mem_w_o,
           ffn2_norm, ffn2_w_gate, ffn2_w_up, ffn2_w_down, final_norm):
    b, s, d = x.shape
    m = mem.shape[1]
    depth = w_in.shape[0]
    sb_width = w_out.shape[1] // 2
    bf = lambda a: a.astype(BF16)
    vec = lambda a: a.reshape(1, -1)

    x2d = x.reshape(b * s, d)
    for l in range(depth):
        last = l == depth - 1
        x2d = _ffn(x2d, vec(ffn1_norm[l]), bf(ffn1_w_gate[l]), bf(ffn1_w_up[l]), bf(ffn1_w_down[l]))
        q, k, v, u = _proj(x2d, vec(mix_norm[l]), bf(w_in[l]), sb_width=sb_width)
        shp = lambda a: a.reshape(b, s, a.shape[-1])
        o_sb = _sb_attention(shp(q), shp(k), shp(v))
        x3d = _mix_out(x2d.reshape(b, s, d), o_sb, shp(u), bf(w_out[l]), bf(w_pool[l]),
                       vec(pool_scale[l]))
        kv = _mem_kv(mem.reshape(b * m, d), vec(mem_kv_norm[l]), bf(mem_w_kv[l]))
        x3d = _xattn(x3d, vec(mem_q_norm[l]), bf(mem_w_q[l]), kv.reshape(b, m, 2 * d), bf(mem_w_o[l]))
        x2d = _ffn(x3d.reshape(b * s, d), vec(ffn2_norm[l]), bf(ffn2_w_gate[l]), bf(ffn2_w_up[l]),
                   bf(ffn2_w_down[l]), vec(final_norm) if last else None)
    return x2d.reshape(b, s, d)
```

```python
import functools

import jax
import jax.numpy as jnp
from jax import lax
from jax.experimental import pallas as pl
from jax.experimental.pallas import tpu as pltpu

F32 = jnp.float32
BF16 = jnp.bfloat16

EPS = 1e-6
FFN_RESIDUAL_WEIGHT = 0.5
SB_HEAD_DIM = 64
SB_HEADS_PER_STEP = 2
POOL_WINDOWS = (2, 4, 8, 16)
POOL_HALO = 16
MEM_HEADS = 4

LANES = 128
SB_TQ = 256
SB_TK = 256
SB_STAGES = 4
VMEM_LIMIT = 60 * 1024 * 1024
LOG2_E = 1.4426950408889634
SB_MASKED = -1e30


def _rms(x, g):
    return x * lax.rsqrt(jnp.mean(x * x, axis=-1, keepdims=True) + EPS) * g


def _resident(shape):
    zeros = (0,) * len(shape)
    return pl.BlockSpec(shape, lambda *_: zeros, pipeline_mode=pl.Buffered(1))


def _params(*sem):
    return pltpu.CompilerParams(dimension_semantics=sem, vmem_limit_bytes=VMEM_LIMIT)


def _ffn_kernel(x_ref, g_ref, wg_ref, wu_ref, wd_ref, *rest, final_norm):
    o_ref = rest[-1]
    x = x_ref[...]
    h = _rms(x, g_ref[...]).astype(BF16)
    gate = jnp.dot(h, wg_ref[...], preferred_element_type=F32)
    up = jnp.dot(h, wu_ref[...], preferred_element_type=F32)
    act = (jax.nn.silu(gate) * up).astype(BF16)
    y = x + FFN_RESIDUAL_WEIGHT * jnp.dot(act, wd_ref[...], preferred_element_type=F32)
    if final_norm:
        y = _rms(y, rest[0][...])
    o_ref[...] = y


def _ffn(x2d, g, wg, wu, wd, gf=None, *, tm=512):
    t, d = x2d.shape
    dff = wg.shape[1]
    row = pl.BlockSpec((tm, d), lambda i: (i, 0))
    vec = _resident((1, d))
    in_specs = [row, vec, _resident((d, dff)), _resident((d, dff)), _resident((dff, d))]
    args = [x2d, g, wg, wu, wd]
    if gf is not None:
        in_specs.append(vec)
        args.append(gf)
    return pl.pallas_call(
        functools.partial(_ffn_kernel, final_norm=gf is not None),
        grid=(t // tm,),
        in_specs=in_specs,
        out_specs=row,
        out_shape=jax.ShapeDtypeStruct((t, d), F32),
        compiler_params=_params("parallel"),
        name="ffn_final" if gf is not None else "ffn",
    )(*args)


def _proj_kernel(x_ref, g_ref, w_ref, q_ref, k_ref, v_ref, u_ref, *, q_scale):
    h = _rms(x_ref[...], g_ref[...]).astype(BF16)
    p = jnp.dot(h, w_ref[...], preferred_element_type=F32)
    w = q_ref.shape[1]
    q_ref[...] = (p[:, :w] * q_scale).astype(BF16)
    k_ref[...] = p[:, w:2 * w].astype(BF16)
    v_ref[...] = p[:, 2 * w:3 * w].astype(BF16)
    u_ref[...] = p[:, 3 * w:]


def _proj(x2d, g, w_in, *, sb_width, tm=512):
    t, d = x2d.shape
    cols = w_in.shape[1]
    pool_width = cols - 3 * sb_width
    row = lambda n: pl.BlockSpec((tm, n), lambda i: (i, 0))
    return pl.pallas_call(
        functools.partial(_proj_kernel, q_scale=LOG2_E * SB_HEAD_DIM ** -0.5),
        grid=(t // tm,),
        in_specs=[row(d), _resident((1, d)), _resident((d, cols))],
        out_specs=[row(sb_width), row(sb_width), row(sb_width), row(pool_width)],
        out_shape=[jax.ShapeDtypeStruct((t, sb_width), BF16)] * 3
        + [jax.ShapeDtypeStruct((t, pool_width), F32)],
        compiler_params=_params("parallel"),
        name="proj",
    )(x2d, g, w_in)


def _sb_scores(z2, carry):
    neg_abs = lax.bitcast_convert_type(
        lax.bitcast_convert_type(z2, jnp.uint32) | jnp.uint32(0x80000000), F32)
    rest = jnp.maximum(z2, 0.0) + jnp.log2(1.0 + jnp.exp2(neg_abs))
    return rest.astype(BF16), z2 - rest - carry, carry + jnp.sum(rest, axis=1, keepdims=True)


def _sb_kernel(tq_ref, td_ref, q_ref, k_ref, v_ref, tri_ref, bias_ref, o_ref,
               z_ref, rest_ref, wgt_ref, lbc_ref, acc_ref, carry_ref):
    n_steps = tq_ref.shape[0] - (SB_STAGES - 1)
    heads = range(SB_HEADS_PER_STEP)
    lane = lax.broadcasted_iota(jnp.int32, (1, LANES), 1)
    head_lanes = [lane < SB_HEAD_DIM, lane >= SB_HEAD_DIM]
    nt = (((1,), (1,)), ((), ()))

    for ref in (z_ref, rest_ref, wgt_ref, lbc_ref, acc_ref, carry_ref):
        ref[...] = jnp.zeros(ref.shape, ref.dtype)

    def tile(g, stage):
        i = g + (SB_STAGES - 1) - stage
        return tq_ref[i], td_ref[i]

    def rows(block):
        return pl.ds(pl.multiple_of(block * SB_TK, SB_TK), SB_TK)

    def step(g, cur):
        prev = 1 - cur
        qi0, d0 = tile(g, 0)
        _, d1 = tile(g, 1)
        qi3, d3 = tile(g, 3)
        q = q_ref[0, rows(qi0), :]
        kb = k_ref[0, rows(qi0 - d0), :]
        vb = v_ref[0, rows(qi3 - d3), :]
        bias = bias_ref[jnp.minimum(d0, 1)]
        z_new = [lax.dot_general(jnp.where(head_lanes[h], q, jnp.zeros_like(q)), kb, nt,
                                 preferred_element_type=F32) + bias for h in heads]
        sufs = [jnp.dot(rest_ref[prev, h], tri_ref[...], preferred_element_type=F32) for h in heads]
        pvs = [jnp.dot(wgt_ref[prev, h], vb, preferred_element_type=F32) for h in heads]
        for h in heads:
            carry = jnp.where(d1 == 0, 0.0, carry_ref[h])
            rest_ref[cur, h], lbc_ref[cur, h], carry_ref[h] = _sb_scores(z_ref[prev, h], carry)
        for h in heads:
            z_ref[cur, h] = z_new[h]
            wgt_ref[cur, h] = jnp.exp2(lbc_ref[prev, h] - sufs[h]).astype(BF16)
            acc_ref[h] = jnp.where(d3 == 0, 0.0, acc_ref[h]) + pvs[h]

        @pl.when(d3 == qi3)
        def _():
            o_ref[0, rows(qi3), :] = jnp.where(head_lanes[0], acc_ref[0], acc_ref[1]).astype(o_ref.dtype)

    def step_pair(p, _):
        step(2 * p, 0)
        step(2 * p + 1, 1)
        return 0

    lax.fori_loop(0, n_steps // 2, step_pair, 0)


def _sb_attention(q, k, v):
    b, s, w = q.shape
    assert SB_TQ == SB_TK and s % SB_TQ == 0 and w % LANES == 0
    assert SB_HEADS_PER_STEP * SB_HEAD_DIM == LANES
    n_q = s // SB_TQ
    tiles = [(qi, d) for qi in range(n_q) for d in range(qi + 1)]
    idle = (n_q - 1, 0)
    assert n_q > 1
    n_steps = len(tiles) + SB_STAGES - 1
    n_steps += n_steps % 2
    table = [idle] * (SB_STAGES - 1) + tiles
    table += [idle] * (n_steps + SB_STAGES - 1 - len(table))
    tq = jnp.array([t[0] for t in table], jnp.int32)
    td = jnp.array([t[1] for t in table], jnp.int32)
    row = lax.broadcasted_iota(jnp.int32, (SB_TQ, SB_TK), 0)
    col = lax.broadcasted_iota(jnp.int32, (SB_TQ, SB_TK), 1)
    tri = (row > col).astype(BF16)
    diag_bias = jnp.where(col < row, 0.0, SB_MASKED).astype(F32)
    bias = jnp.stack([diag_bias, jnp.zeros_like(diag_bias)])
    blk = pl.BlockSpec((1, s, LANES), lambda i, j: (i, 0, j))
    smem = pl.BlockSpec(memory_space=pltpu.SMEM)
    per_head = lambda cols, dtype: pltpu.VMEM((SB_HEADS_PER_STEP, SB_TQ, cols), dtype)
    per_parity = lambda dtype: pltpu.VMEM((2, SB_HEADS_PER_STEP, SB_TQ, SB_TK), dtype)
    return pl.pallas_call(
        _sb_kernel,
        grid=(b, w // LANES),
        in_specs=[smem, smem, blk, blk, blk, _resident(tri.shape), _resident(bias.shape)],
        out_specs=blk,
        out_shape=jax.ShapeDtypeStruct((b, s, w), BF16),
        scratch_shapes=[per_parity(F32), per_parity(BF16), per_parity(BF16), per_parity(F32),
                        per_head(LANES, F32), per_head(1, F32)],
        compiler_params=_params("parallel", "parallel"),
        name="sb_attn",
    )(tq, td, q, k, v, tri, bias)


def _mixout_kernel(x_ref, osb_ref, u_ref, wout_ref, wpool_ref, ps_ref, o_ref, ubuf):
    ts = u_ref.shape[1]
    sb_width = osb_ref.shape[2]
    gdim = wpool_ref.shape[1]
    si = pl.program_id(1)

    @pl.when(si == 0)
    def _():
        ubuf[0:POOL_HALO, :] = jnp.zeros((POOL_HALO, ubuf.shape[1]), F32)

    ubuf[POOL_HALO:, :] = u_ref[0]
    pos = si * ts + lax.broadcasted_iota(jnp.int32, (ts, 1), 0)
    pooled_out = []
    for g, w in enumerate(POOL_WINDOWS):
        cols = slice(g * gdim, (g + 1) * gdim)
        token = ubuf[POOL_HALO:, cols]
        total = token
        for d in range(1, w):
            total = total + ubuf[POOL_HALO - d:POOL_HALO - d + ts, cols]
        inv_count = 1.0 / jnp.minimum(pos + 1, w).astype(F32)
        pooled = (total * inv_count - token).astype(BF16)
        pooled_out.append(jnp.dot(pooled, wpool_ref[g], preferred_element_type=F32))
    o_pool = (jnp.concatenate(pooled_out, axis=1) * ps_ref[...]).astype(BF16)
    ubuf[0:POOL_HALO, :] = ubuf[ts:ts + POOL_HALO, :]
    y = x_ref[0] + jnp.dot(osb_ref[0], wout_ref[0:sb_width, :], preferred_element_type=F32)
    y = y + jnp.dot(o_pool, wout_ref[sb_width:, :], preferred_element_type=F32)
    o_ref[0] = y


def _mix_out(x, o_sb, u, w_out, w_pool, pool_scale, *, ts=512):
    b, s, d = x.shape
    sb_width = o_sb.shape[2]
    pool_width = u.shape[2]
    assert POOL_HALO >= max(POOL_WINDOWS) - 1 and ts >= POOL_HALO
    blk = lambda n: pl.BlockSpec((1, ts, n), lambda i, j: (i, j, 0))
    return pl.pallas_call(
        _mixout_kernel,
        grid=(b, s // ts),
        in_specs=[blk(d), blk(sb_width), blk(pool_width), _resident(w_out.shape),
                  _resident(w_pool.shape), _resident((1, pool_width))],
        out_specs=blk(d),
        out_shape=jax.ShapeDtypeStruct((b, s, d), F32),
        scratch_shapes=[pltpu.VMEM((ts + POOL_HALO, pool_width), F32)],
        compiler_params=_params("parallel", "arbitrary"),
        name="mix_out",
    )(x, o_sb, u, w_out, w_pool, pool_scale)


def _memkv_kernel(m_ref, g_ref, w_ref, o_ref):
    h = _rms(m_ref[...], g_ref[...]).astype(BF16)
    o_ref[...] = jnp.dot(h, w_ref[...], preferred_element_type=F32).astype(o_ref.dtype)


def _mem_kv(mem2d, g, w_kv, *, tm=1024):
    t, d = mem2d.shape
    n = w_kv.shape[1]
    return pl.pallas_call(
        _memkv_kernel,
        grid=(t // tm,),
        in_specs=[pl.BlockSpec((tm, d), lambda i: (i, 0)), _resident((1, d)), _resident((d, n))],
        out_specs=pl.BlockSpec((tm, n), lambda i: (i, 0)),
        out_shape=jax.ShapeDtypeStruct((t, n), BF16),
        compiler_params=_params("parallel"),
        name="mem_kv",
    )(mem2d, g, w_kv)


def _xattn_kernel(x_ref, g_ref, wq_ref, kv_ref, wo_ref, o_ref):
    x = x_ref[0]
    d = x.shape[1]
    hd = d // MEM_HEADS
    h = _rms(x, g_ref[...]).astype(BF16)
    q = (jnp.dot(h, wq_ref[...], preferred_element_type=F32) * hd ** -0.5).astype(BF16)
    heads = []
    for i in range(MEM_HEADS):
        qh = q[:, i * hd:(i + 1) * hd]
        kh = kv_ref[0, :, i * hd:(i + 1) * hd]
        vh = kv_ref[0, :, d + i * hd:d + (i + 1) * hd]
        s = lax.dot_general(qh, kh, (((1,), (1,)), ((), ())), preferred_element_type=F32)
        p = jnp.exp(s - jnp.max(s, axis=-1, keepdims=True))
        inv = 1.0 / jnp.sum(p, axis=-1, keepdims=True)
        heads.append((jnp.dot(p.astype(BF16), vh, preferred_element_type=F32) * inv).astype(BF16))
    o = jnp.concatenate(heads, axis=1)
    o_ref[0] = x + jnp.dot(o, wo_ref[...], preferred_element_type=F32)


def _xattn(x, g, w_q, kv, w_o, *, ts=512):
    b, s, d = x.shape
    m = kv.shape[1]
    blk = pl.BlockSpec((1, ts, d), lambda i, j: (i, j, 0))
    return pl.pallas_call(
        _xattn_kernel,
        grid=(b, s // ts),
        in_specs=[blk, _resident((1, d)), _resident((d, d)),
                  pl.BlockSpec((1, m, 2 * d), lambda i, j: (i, 0, 0)), _resident((d, d))],
        out_specs=blk,
        out_shape=jax.ShapeDtypeStruct((b, s, d), F32),
        compiler_params=_params("parallel", "parallel"),
        name="xattn",
    )(x, g, w_q, kv, w_o)


def kernel(x, mem, ffn1_norm, ffn1_w_gate, ffn1_w_up, ffn1_w_down, mix_norm, w_in, w_pool,
           pool_scale, w_out, mem_q_norm, mem_kv_norm, mem_w_q, mem_w_kv, mem_w_o,
           ffn2_norm, ffn2_w_gate, ffn2_w_up, ffn2_w_down, final_norm):
    b, s, d = x.shape
    m = mem.shape[1]
    depth = w_in.shape[0]
    sb_width = w_out.shape[1] // 2
    bf = lambda a: a.astype(BF16)
    vec = lambda a: a.reshape(1, -1)

    x2d = x.reshape(b * s, d)
    for l in range(depth):
        last = l == depth - 1
        x2d = _ffn(x2d, vec(ffn1_norm[l]), bf(ffn1_w_gate[l]), bf(ffn1_w_up[l]), bf(ffn1_w_down[l]))
        q, k, v, u = _proj(x2d, vec(mix_norm[l]), bf(w_in[l]), sb_width=sb_width)
        shp = lambda a: a.reshape(b, s, a.shape[-1])
        o_sb = _sb_attention(shp(q), shp(k), shp(v))
        x3d = _mix_out(x2d.reshape(b, s, d), o_sb, shp(u), bf(w_out[l]), bf(w_pool[l]),
                       vec(pool_scale[l]))
        kv = _mem_kv(mem.reshape(b * m, d), vec(mem_kv_norm[l]), bf(mem_w_kv[l]))
        x3d = _xattn(x3d, vec(mem_q_norm[l]), bf(mem_w_q[l]), kv.reshape(b, m, 2 * d), bf(mem_w_o[l]))
        x2d = _ffn(x3d.reshape(b * s, d), vec(ffn2_norm[l]), bf(ffn2_w_gate[l]), bf(ffn2_w_up[l]),
                   bf(ffn2_w_down[l]), vec(final_norm) if last else None)
    return x2d.reshape(b, s, d)
```

```python
import functools

import jax
import jax.numpy as jnp
from jax import lax
from jax.experimental import pallas as pl
from jax.experimental.pallas import tpu as pltpu

F32 = jnp.float32
BF16 = jnp.bfloat16

EPS = 1e-6
FFN_RESIDUAL_WEIGHT = 0.5
SB_HEAD_DIM = 64
SB_HEADS_PER_STEP = 2
POOL_WINDOWS = (2, 4, 8, 16)
POOL_HALO = 16
MEM_HEADS = 4

LANES = 128
SUBLANES = 8
SB_TQ = 256
SB_TK = 256
SB_STAGES = 4
VMEM_LIMIT = 60 * 1024 * 1024
LOG2_E = 1.4426950408889634
SB_MASKED = -1e30
SB_DEAD_CARRY = 150.0


def _rms(x, g):
    return x * lax.rsqrt(jnp.mean(x * x, axis=-1, keepdims=True) + EPS) * g


def _resident(shape):
    zeros = (0,) * len(shape)
    return pl.BlockSpec(shape, lambda *_: zeros, pipeline_mode=pl.Buffered(1))


def _params(*sem):
    return pltpu.CompilerParams(dimension_semantics=sem, vmem_limit_bytes=VMEM_LIMIT)


def _ffn_kernel(x_ref, g_ref, wg_ref, wu_ref, wd_ref, *rest, final_norm):
    o_ref = rest[-1]
    x = x_ref[...]
    h = _rms(x, g_ref[...]).astype(BF16)
    gate = jnp.dot(h, wg_ref[...], preferred_element_type=F32)
    up = jnp.dot(h, wu_ref[...], preferred_element_type=F32)
    act = (jax.nn.silu(gate) * up).astype(BF16)
    y = x + FFN_RESIDUAL_WEIGHT * jnp.dot(act, wd_ref[...], preferred_element_type=F32)
    if final_norm:
        y = _rms(y, rest[0][...])
    o_ref[...] = y


def _ffn(x2d, g, wg, wu, wd, gf=None, *, tm=512):
    t, d = x2d.shape
    dff = wg.shape[1]
    row = pl.BlockSpec((tm, d), lambda i: (i, 0))
    vec = _resident((1, d))
    in_specs = [row, vec, _resident((d, dff)), _resident((d, dff)), _resident((dff, d))]
    args = [x2d, g, wg, wu, wd]
    if gf is not None:
        in_specs.append(vec)
        args.append(gf)
    return pl.pallas_call(
        functools.partial(_ffn_kernel, final_norm=gf is not None),
        grid=(t // tm,),
        in_specs=in_specs,
        out_specs=row,
        out_shape=jax.ShapeDtypeStruct((t, d), F32),
        compiler_params=_params("parallel"),
        name="ffn_final" if gf is not None else "ffn",
    )(*args)


def _proj_kernel(x_ref, g_ref, w_ref, q_ref, k_ref, v_ref, u_ref, *, q_scale):
    h = _rms(x_ref[...], g_ref[...]).astype(BF16)
    p = jnp.dot(h, w_ref[...], preferred_element_type=F32)
    w = q_ref.shape[1]
    q_ref[...] = (p[:, :w] * q_scale).astype(BF16)
    k_ref[...] = p[:, w:2 * w].astype(BF16)
    v_ref[...] = p[:, 2 * w:3 * w].astype(BF16)
    u_ref[...] = p[:, 3 * w:]


def _proj(x2d, g, w_in, *, sb_width, tm=512):
    t, d = x2d.shape
    cols = w_in.shape[1]
    pool_width = cols - 3 * sb_width
    row = lambda n: pl.BlockSpec((tm, n), lambda i: (i, 0))
    return pl.pallas_call(
        functools.partial(_proj_kernel, q_scale=LOG2_E * SB_HEAD_DIM ** -0.5),
        grid=(t // tm,),
        in_specs=[row(d), _resident((1, d)), _resident((d, cols))],
        out_specs=[row(sb_width), row(sb_width), row(sb_width), row(pool_width)],
        out_shape=[jax.ShapeDtypeStruct((t, sb_width), BF16)] * 3
        + [jax.ShapeDtypeStruct((t, pool_width), F32)],
        compiler_params=_params("parallel"),
        name="proj",
    )(x2d, g, w_in)


def _sb_scores(z2, carry):
    pos = jnp.maximum(z2, 0.0)
    neg = z2 - pos
    soft = jnp.log2(1.0 + jnp.exp2(neg - pos))
    rest = pos + soft
    return rest.astype(BF16), neg - soft - carry, carry + jnp.sum(rest, axis=1, keepdims=True)


def _sb_kernel(q_ref, k_ref, v_ref, tri_ref, bias_ref, o_ref,
               z_ref, rest_ref, wgt_ref, lbc_ref, acc_ref, carry_ref):
    n_q = q_ref.shape[1] // SB_TQ
    heads = range(SB_HEADS_PER_STEP)
    lane = lax.broadcasted_iota(jnp.int32, (1, LANES), 1)
    head_lanes = [lane < SB_HEAD_DIM, lane >= SB_HEAD_DIM]
    nt = (((1,), (1,)), ((), ()))

    for ref in (z_ref, rest_ref, wgt_ref, lbc_ref, acc_ref, carry_ref):
        ref[...] = jnp.zeros(ref.shape, ref.dtype)

    def rows(block):
        return pl.ds(pl.multiple_of(block * SB_TK, SB_TK), SB_TK)

    def step(cur, state):
        (gen_qi, gen_d), (qi1, d1, _), stage2, (qi3, d3, last3), drained = state
        prev = 1 - cur
        real = gen_qi < n_q
        qi0 = jnp.where(real, gen_qi, n_q - 1)
        d0 = jnp.where(real, gen_d, 0)
        q = q_ref[0, rows(qi0), :]
        kb = k_ref[0, rows(qi0 - d0), :]
        vb = v_ref[0, rows(qi3 - d3), :]
        bias = bias_ref[jnp.minimum(d0, 1)]
        z_new = [lax.dot_general(jnp.where(head_lanes[h], q, jnp.zeros_like(q)), kb, nt,
                                 preferred_element_type=F32) + bias for h in heads]
        sufs = [jnp.dot(rest_ref[prev, h], tri_ref[...], preferred_element_type=F32) for h in heads]
        pvs = [jnp.dot(wgt_ref[prev, h], vb, preferred_element_type=F32) for h in heads]
        carries = []
        for h in heads:
            carry = jnp.where(d1 == 0, 0.0, carry_ref[h])
            rest_ref[cur, h], lbc_ref[cur, h], carry = _sb_scores(z_ref[prev, h], carry)
            carry_ref[h] = carry
            carries.append(carry)
        for h in heads:
            z_ref[cur, h] = z_new[h]
            wgt_ref[cur, h] = jnp.exp2(lbc_ref[prev, h] - sufs[h]).astype(BF16)
            acc_ref[h] = jnp.where(d3 == 0, 0.0, acc_ref[h]) + pvs[h]

        low = jnp.min(jnp.minimum(*carries).reshape(SB_TQ // SUBLANES, SUBLANES, 1), axis=0)
        for shift in (4, 2, 1):
            low = jnp.minimum(low, pltpu.roll(low, shift, axis=0))
        dead = low[0, 0] >= SB_DEAD_CARRY

        @pl.when(last3 == 1)
        def _():
            o_ref[0, rows(qi3), :] = jnp.where(head_lanes[0], acc_ref[0], acc_ref[1]).astype(o_ref.dtype)

        last0 = real & ((d0 == qi0) | (dead & (qi1 == qi0)))
        nxt = (jnp.where(last0, gen_qi + 1, gen_qi), jnp.where(last0, 0, gen_d + real.astype(jnp.int32)))
        return (nxt, (qi0, d0, last0.astype(jnp.int32)), state[1], stage2,
                drained + 1 - real.astype(jnp.int32))

    def step_pair(state):
        return step(1, step(0, state))

    idle = (jnp.int32(n_q - 1), jnp.int32(0), jnp.int32(0))
    state = ((jnp.int32(0), jnp.int32(0)), idle, idle, idle, jnp.int32(0))
    lax.while_loop(lambda st: st[-1] < SB_STAGES - 1, step_pair, state)


def _sb_attention(q, k, v):
    b, s, w = q.shape
    assert SB_TQ == SB_TK and s % SB_TQ == 0 and w % LANES == 0
    assert SB_HEADS_PER_STEP * SB_HEAD_DIM == LANES
    assert s // SB_TQ > 1
    row = lax.broadcasted_iota(jnp.int32, (SB_TQ, SB_TK), 0)
    col = lax.broadcasted_iota(jnp.int32, (SB_TQ, SB_TK), 1)
    tri = (row > col).astype(BF16)
    diag_bias = jnp.where(col < row, 0.0, SB_MASKED).astype(F32)
    bias = jnp.stack([diag_bias, jnp.zeros_like(diag_bias)])
    blk = pl.BlockSpec((1, s, LANES), lambda i, j: (i, 0, j))
    per_head = lambda cols, dtype: pltpu.VMEM((SB_HEADS_PER_STEP, SB_TQ, cols), dtype)
    per_parity = lambda dtype: pltpu.VMEM((2, SB_HEADS_PER_STEP, SB_TQ, SB_TK), dtype)
    return pl.pallas_call(
        _sb_kernel,
        grid=(b, w // LANES),
        in_specs=[blk, blk, blk, _resident(tri.shape), _resident(bias.shape)],
        out_specs=blk,
        out_shape=jax.ShapeDtypeStruct((b, s, w), BF16),
        scratch_shapes=[per_parity(F32), per_parity(BF16), per_parity(BF16), per_parity(F32),
                        per_head(LANES, F32), per_head(1, F32)],
        compiler_params=_params("parallel", "parallel"),
        name="sb_attn",
    )(q, k, v, tri, bias)


def _pool_mix(u_ref, ubuf, wpool_ref, ps_ref, si):
    ts = u_ref.shape[1]
    gdim = wpool_ref.shape[1]

    @pl.when(si == 0)
    def _():
        ubuf[0:POOL_HALO, :] = jnp.zeros((POOL_HALO, ubuf.shape[1]), F32)

    ubuf[POOL_HALO:, :] = u_ref[0]
    pos = si * ts + lax.broadcasted_iota(jnp.int32, (ts, 1), 0)
    pooled_out = []
    for g, w in enumerate(POOL_WINDOWS):
        cols = slice(g * gdim, (g + 1) * gdim)
        token = ubuf[POOL_HALO:, cols]
        total = token
        for d in range(1, w):
            total = total + ubuf[POOL_HALO - d:POOL_HALO - d + ts, cols]
        inv_count = 1.0 / jnp.minimum(pos + 1, w).astype(F32)
        pooled = (total * inv_count - token).astype(BF16)
        pooled_out.append(jnp.dot(pooled, wpool_ref[g], preferred_element_type=F32))
    o_pool = (jnp.concatenate(pooled_out, axis=1) * ps_ref[...]).astype(BF16)
    ubuf[0:POOL_HALO, :] = ubuf[ts:ts + POOL_HALO, :]
    return o_pool


def _cross_attention(x, g_ref, wq_ref, kv_ref, wo_ref):
    d = x.shape[1]
    hd = d // MEM_HEADS
    h = _rms(x, g_ref[...]).astype(BF16)
    q = (jnp.dot(h, wq_ref[...], preferred_element_type=F32) * hd ** -0.5).astype(BF16)
    heads = []
    for i in range(MEM_HEADS):
        qh = q[:, i * hd:(i + 1) * hd]
        kh = kv_ref[0, :, i * hd:(i + 1) * hd]
        vh = kv_ref[0, :, d + i * hd:d + (i + 1) * hd]
        s = lax.dot_general(qh, kh, (((1,), (1,)), ((), ())), preferred_element_type=F32)
        p = jnp.exp(s - jnp.max(s, axis=-1, keepdims=True))
        inv = 1.0 / jnp.sum(p, axis=-1, keepdims=True)
        heads.append((jnp.dot(p.astype(BF16), vh, preferred_element_type=F32) * inv).astype(BF16))
    o = jnp.concatenate(heads, axis=1)
    return jnp.dot(o, wo_ref[...], preferred_element_type=F32)


def _mix_xattn_kernel(x_ref, osb_ref, u_ref, wout_ref, wpool_ref, ps_ref, g_ref, wq_ref, kv_ref,
                      wo_ref, o_ref, ubuf):
    sb_width = osb_ref.shape[2]
    o_pool = _pool_mix(u_ref, ubuf, wpool_ref, ps_ref, pl.program_id(1))
    y = x_ref[0] + jnp.dot(osb_ref[0], wout_ref[0:sb_width, :], preferred_element_type=F32)
    y = y + jnp.dot(o_pool, wout_ref[sb_width:, :], preferred_element_type=F32)
    o_ref[0] = y + _cross_attention(y, g_ref, wq_ref, kv_ref, wo_ref)


def _mix_xattn(x, o_sb, u, w_out, w_pool, pool_scale, g, w_q, kv, w_o, *, ts=512):
    b, s, d = x.shape
    m = kv.shape[1]
    sb_width = o_sb.shape[2]
    pool_width = u.shape[2]
    assert POOL_HALO >= max(POOL_WINDOWS) - 1 and ts >= POOL_HALO
    blk = lambda n: pl.BlockSpec((1, ts, n), lambda i, j: (i, j, 0))
    return pl.pallas_call(
        _mix_xattn_kernel,
        grid=(b, s // ts),
        in_specs=[blk(d), blk(sb_width), blk(pool_width), _resident(w_out.shape),
                  _resident(w_pool.shape), _resident((1, pool_width)), _resident((1, d)),
                  _resident((d, d)), pl.BlockSpec((1, m, 2 * d), lambda i, j: (i, 0, 0)),
                  _resident((d, d))],
        out_specs=blk(d),
        out_shape=jax.ShapeDtypeStruct((b, s, d), F32),
        scratch_shapes=[pltpu.VMEM((ts + POOL_HALO, pool_width), F32)],
        compiler_params=_params("parallel", "arbitrary"),
        name="mix_xattn",
    )(x, o_sb, u, w_out, w_pool, pool_scale, g, w_q, kv, w_o)


def _memkv_kernel(m_ref, g_ref, w_ref, o_ref):
    h = _rms(m_ref[...], g_ref[...]).astype(BF16)
    o_ref[...] = jnp.dot(h, w_ref[...], preferred_element_type=F32).astype(o_ref.dtype)


def _mem_kv(mem2d, g, w_kv, *, tm=1024):
    t, d = mem2d.shape
    n = w_kv.shape[1]
    return pl.pallas_call(
        _memkv_kernel,
        grid=(t // tm,),
        in_specs=[pl.BlockSpec((tm, d), lambda i: (i, 0)), _resident((1, d)), _resident((d, n))],
        out_specs=pl.BlockSpec((tm, n), lambda i: (i, 0)),
        out_shape=jax.ShapeDtypeStruct((t, n), BF16),
        compiler_params=_params("parallel"),
        name="mem_kv",
    )(mem2d, g, w_kv)


def kernel(x, mem, ffn1_norm, ffn1_w_gate, ffn1_w_up, ffn1_w_down, mix_norm, w_in, w_pool,
           pool_scale, w_out, mem_q_norm, mem_kv_norm, mem_w_q, mem_w_kv, mem_w_o,
           ffn2_norm, ffn2_w_gate, ffn2_w_up, ffn2_w_down, final_norm):
    b, s, d = x.shape
    m = mem.shape[1]
    depth = w_in.shape[0]
    sb_width = w_out.shape[1] // 2
    bf = lambda a: a.astype(BF16)
    vec = lambda a: a.reshape(1, -1)

    x2d = x.reshape(b * s, d)
    for l in range(depth):
        last = l == depth - 1
        x2d = _ffn(x2d, vec(ffn1_norm[l]), bf(ffn1_w_gate[l]), bf(ffn1_w_up[l]), bf(ffn1_w_down[l]))
        q, k, v, u = _proj(x2d, vec(mix_norm[l]), bf(w_in[l]), sb_width=sb_width)
        shp = lambda a: a.reshape(b, s, a.shape[-1])
        o_sb = _sb_attention(shp(q), shp(k), shp(v))
        kv = _mem_kv(mem.reshape(b * m, d), vec(mem_kv_norm[l]), bf(mem_w_kv[l]))
        x3d = _mix_xattn(x2d.reshape(b, s, d), o_sb, shp(u), bf(w_out[l]), bf(w_pool[l]),
                         vec(pool_scale[l]), vec(mem_q_norm[l]), bf(mem_w_q[l]),
                         kv.reshape(b, m, 2 * d), bf(mem_w_o[l]))
        x2d = _ffn(x3d.reshape(b * s, d), vec(ffn2_norm[l]), bf(ffn2_w_gate[l]), bf(ffn2_w_up[l]),
                   bf(ffn2_w_down[l]), vec(final_norm) if last else None)
    return x2d.reshape(b, s, d)
```

```python
import functools

import jax
import jax.numpy as jnp
from jax import lax
from jax.experimental import pallas as pl
from jax.experimental.pallas import tpu as pltpu

F32 = jnp.float32
BF16 = jnp.bfloat16

EPS = 1e-6
FFN_RESIDUAL_WEIGHT = 0.5
SB_HEAD_DIM = 64
SB_HEADS_PER_STEP = 2
POOL_WINDOWS = (2, 4, 8, 16)
POOL_HALO = 16
MEM_HEADS = 4

LANES = 128
SUBLANES = 8
SB_TQ = 256
SB_TK = 256
SB_STAGES = 4
VMEM_LIMIT = 60 * 1024 * 1024
LOG2_E = 1.4426950408889634
SB_MASKED = -1e30
SB_DEAD_CARRY = 150.0


def _rms(x, g):
    return x * lax.rsqrt(jnp.mean(x * x, axis=-1, keepdims=True) + EPS) * g


def _resident(shape):
    zeros = (0,) * len(shape)
    return pl.BlockSpec(shape, lambda *_: zeros, pipeline_mode=pl.Buffered(1))


def _params(*sem):
    return pltpu.CompilerParams(dimension_semantics=sem, vmem_limit_bytes=VMEM_LIMIT)


def _ffn_kernel(x_ref, g_ref, wg_ref, wu_ref, wd_ref, *rest, final_norm):
    o_ref = rest[-1]
    x = x_ref[...]
    h = _rms(x, g_ref[...]).astype(BF16)
    gate = jnp.dot(h, wg_ref[...], preferred_element_type=F32)
    up = jnp.dot(h, wu_ref[...], preferred_element_type=F32)
    act = (jax.nn.silu(gate) * up).astype(BF16)
    y = x + FFN_RESIDUAL_WEIGHT * jnp.dot(act, wd_ref[...], preferred_element_type=F32)
    if final_norm:
        y = _rms(y, rest[0][...])
    o_ref[...] = y


def _ffn(x2d, g, wg, wu, wd, gf=None, *, tm=512):
    t, d = x2d.shape
    dff = wg.shape[1]
    row = pl.BlockSpec((tm, d), lambda i: (i, 0))
    vec = _resident((1, d))
    in_specs = [row, vec, _resident((d, dff)), _resident((d, dff)), _resident((dff, d))]
    args = [x2d, g, wg, wu, wd]
    if gf is not None:
        in_specs.append(vec)
        args.append(gf)
    return pl.pallas_call(
        functools.partial(_ffn_kernel, final_norm=gf is not None),
        grid=(t // tm,),
        in_specs=in_specs,
        out_specs=row,
        out_shape=jax.ShapeDtypeStruct((t, d), F32),
        compiler_params=_params("parallel"),
        name="ffn_final" if gf is not None else "ffn",
    )(*args)


def _proj_kernel(x_ref, g_ref, w_ref, q_ref, k_ref, v_ref, u_ref, *, q_scale):
    h = _rms(x_ref[...], g_ref[...]).astype(BF16)
    p = jnp.dot(h, w_ref[...], preferred_element_type=F32)
    w = q_ref.shape[1]
    q_ref[...] = (p[:, :w] * q_scale).astype(BF16)
    k_ref[...] = p[:, w:2 * w].astype(BF16)
    v_ref[...] = p[:, 2 * w:3 * w].astype(BF16)
    u_ref[...] = p[:, 3 * w:]


def _proj(x2d, g, w_in, *, sb_width, tm=512):
    t, d = x2d.shape
    cols = w_in.shape[1]
    pool_width = cols - 3 * sb_width
    row = lambda n: pl.BlockSpec((tm, n), lambda i: (i, 0))
    return pl.pallas_call(
        functools.partial(_proj_kernel, q_scale=LOG2_E * SB_HEAD_DIM ** -0.5),
        grid=(t // tm,),
        in_specs=[row(d), _resident((1, d)), _resident((d, cols))],
        out_specs=[row(sb_width), row(sb_width), row(sb_width), row(pool_width)],
        out_shape=[jax.ShapeDtypeStruct((t, sb_width), BF16)] * 3
        + [jax.ShapeDtypeStruct((t, pool_width), F32)],
        compiler_params=_params("parallel"),
        name="proj",
    )(x2d, g, w_in)


def _sb_scores(z2, carry):
    pos = jnp.maximum(z2, 0.0)
    neg = z2 - pos
    soft = jnp.log2(1.0 + jnp.exp2(neg - pos))
    rest = pos + soft
    return rest.astype(BF16), neg - soft - carry, carry + jnp.sum(rest, axis=1, keepdims=True)


def _sb_kernel(q_ref, k_ref, v_ref, tri_ref, bias_ref, o_ref,
               z_ref, rest_ref, wgt_ref, lbc_ref, acc_ref, carry_ref):
    n_q = q_ref.shape[1] // SB_TQ
    heads = range(SB_HEADS_PER_STEP)
    lane = lax.broadcasted_iota(jnp.int32, (1, LANES), 1)
    head_lanes = [lane < SB_HEAD_DIM, lane >= SB_HEAD_DIM]
    nt = (((1,), (1,)), ((), ()))
    i32 = lambda b: b.astype(jnp.int32)

    for ref in (z_ref, rest_ref, wgt_ref, lbc_ref):
        ref[1] = jnp.zeros(ref.shape[1:], ref.dtype)
    for ref in (acc_ref, carry_ref):
        ref[...] = jnp.zeros(ref.shape, ref.dtype)

    def rows(block):
        return pl.ds(pl.multiple_of(block * SB_TK, SB_TK), SB_TK)

    def step(cur, state):
        gens, (qi1, d1, last1), stage2, (qi3, d3, last3), drained = state
        prev = 1 - cur
        gen_qi, gen_d = gens[cur]
        real = gen_qi < n_q
        qi0 = jnp.where(real, gen_qi, n_q - 1)
        d0 = jnp.where(real, gen_d, 0)
        q = q_ref[0, rows(qi0), :]
        kb = k_ref[0, rows(qi0 - d0), :]
        vb = v_ref[0, rows(qi3 - d3), :]
        bias = bias_ref[jnp.minimum(d0, 1)]
        z_new = [lax.dot_general(jnp.where(head_lanes[h], q, jnp.zeros_like(q)), kb, nt,
                                 preferred_element_type=F32) + bias for h in heads]
        sufs = [jnp.dot(rest_ref[prev, h], tri_ref[...], preferred_element_type=F32) for h in heads]
        pvs = [jnp.dot(wgt_ref[prev, h], vb, preferred_element_type=F32) for h in heads]
        carries = []
        for h in heads:
            carry = jnp.where(d1 == 0, 0.0, carry_ref[prev, h])
            rest_ref[cur, h], lbc_ref[cur, h], carry = _sb_scores(z_ref[prev, h], carry)
            carry_ref[prev, h] = carry
            carries.append(carry)
        for h in heads:
            z_ref[cur, h] = z_new[h]
            wgt_ref[cur, h] = jnp.exp2(lbc_ref[prev, h] - sufs[h]).astype(BF16)
            acc_ref[prev, h] = jnp.where(d3 == 0, 0.0, acc_ref[prev, h]) + pvs[h]

        low = jnp.min(jnp.minimum(*carries).reshape(SB_TQ // SUBLANES, SUBLANES, 1), axis=0)
        for shift in (4, 2, 1):
            low = jnp.minimum(low, pltpu.roll(low, shift, axis=0))
        dead = low[0, 0] >= SB_DEAD_CARRY

        @pl.when(last3 == 1)
        def _():
            o_ref[0, rows(qi3), :] = jnp.where(
                head_lanes[0], acc_ref[prev, 0], acc_ref[prev, 1]).astype(o_ref.dtype)

        last0 = real & (d0 == qi0)
        gen_cur = (jnp.where(last0, gen_qi + 2, gen_qi), jnp.where(last0, 0, gen_d + i32(real)))
        oth_qi, oth_d = gens[prev]
        cut = dead & (oth_qi == qi1)
        gen_oth = (jnp.where(cut, oth_qi + 2, oth_qi), jnp.where(cut, 0, oth_d))
        new_gens = (gen_cur, gen_oth) if cur == 0 else (gen_oth, gen_cur)
        both_out = (new_gens[0][0] >= n_q) & (new_gens[1][0] >= n_q)
        return (new_gens, (qi0, d0, i32(last0)), (qi1, d1, last1 | i32(cut)), stage2,
                jnp.where(both_out, drained + 1, 0))

    def step_pair(state):
        return step(1, step(0, state))

    zero = jnp.int32(0)
    idle = (jnp.int32(n_q - 1), zero, zero)
    state = (((zero, zero), (jnp.int32(1), zero)), idle, idle, idle, zero)
    lax.while_loop(lambda st: st[-1] < SB_STAGES, step_pair, state)


def _sb_attention(q, k, v):
    b, s, w = q.shape
    assert SB_TQ == SB_TK and s % SB_TQ == 0 and w % LANES == 0
    assert SB_HEADS_PER_STEP * SB_HEAD_DIM == LANES
    assert s // SB_TQ > 1
    row = lax.broadcasted_iota(jnp.int32, (SB_TQ, SB_TK), 0)
    col = lax.broadcasted_iota(jnp.int32, (SB_TQ, SB_TK), 1)
    tri = (row > col).astype(BF16)
    diag_bias = jnp.where(col < row, 0.0, SB_MASKED).astype(F32)
    bias = jnp.stack([diag_bias, jnp.zeros_like(diag_bias)])
    blk = pl.BlockSpec((1, s, LANES), lambda i, j: (i, 0, j))
    per_stream = lambda cols, dtype: pltpu.VMEM((2, SB_HEADS_PER_STEP, SB_TQ, cols), dtype)
    per_parity = lambda dtype: pltpu.VMEM((2, SB_HEADS_PER_STEP, SB_TQ, SB_TK), dtype)
    return pl.pallas_call(
        _sb_kernel,
        grid=(b, w // LANES),
        in_specs=[blk, blk, blk, _resident(tri.shape), _resident(bias.shape)],
        out_specs=blk,
        out_shape=jax.ShapeDtypeStruct((b, s, w), BF16),
        scratch_shapes=[per_parity(F32), per_parity(BF16), per_parity(BF16), per_parity(F32),
                        per_stream(LANES, F32), per_stream(1, F32)],
        compiler_params=_params("parallel", "parallel"),
        name="sb_attn",
    )(q, k, v, tri, bias)


def _pool_mix(u_ref, ubuf, wpool_ref, ps_ref, si):
    ts = u_ref.shape[1]
    gdim = wpool_ref.shape[1]

    @pl.when(si == 0)
    def _():
        ubuf[0:POOL_HALO, :] = jnp.zeros((POOL_HALO, ubuf.shape[1]), F32)

    ubuf[POOL_HALO:, :] = u_ref[0]
    pos = si * ts + lax.broadcasted_iota(jnp.int32, (ts, 1), 0)
    pooled_out = []
    for g, w in enumerate(POOL_WINDOWS):
        cols = slice(g * gdim, (g + 1) * gdim)
        token = ubuf[POOL_HALO:, cols]
        total = token
        for d in range(1, w):
            total = total + ubuf[POOL_HALO - d:POOL_HALO - d + ts, cols]
        inv_count = 1.0 / jnp.minimum(pos + 1, w).astype(F32)
        pooled = (total * inv_count - token).astype(BF16)
        pooled_out.append(jnp.dot(pooled, wpool_ref[g], preferred_element_type=F32))
    o_pool = (jnp.concatenate(pooled_out, axis=1) * ps_ref[...]).astype(BF16)
    ubuf[0:POOL_HALO, :] = ubuf[ts:ts + POOL_HALO, :]
    return o_pool


def _cross_attention(x, g_ref, wq_ref, kv_ref, wo_ref):
    d = x.shape[1]
    hd = d // MEM_HEADS
    h = _rms(x, g_ref[...]).astype(BF16)
    q = (jnp.dot(h, wq_ref[...], preferred_element_type=F32) * hd ** -0.5).astype(BF16)
    heads = []
    for i in range(MEM_HEADS):
        qh = q[:, i * hd:(i + 1) * hd]
        kh = kv_ref[0, :, i * hd:(i + 1) * hd]
        vh = kv_ref[0, :, d + i * hd:d + (i + 1) * hd]
        s = lax.dot_general(qh, kh, (((1,), (1,)), ((), ())), preferred_element_type=F32)
        p = jnp.exp(s - jnp.max(s, axis=-1, keepdims=True))
        inv = 1.0 / jnp.sum(p, axis=-1, keepdims=True)
        heads.append((jnp.dot(p.astype(BF16), vh, preferred_element_type=F32) * inv).astype(BF16))
    o = jnp.concatenate(heads, axis=1)
    return jnp.dot(o, wo_ref[...], preferred_element_type=F32)


def _mix_xattn_kernel(x_ref, osb_ref, u_ref, wout_ref, wpool_ref, ps_ref, g_ref, wq_ref, kv_ref,
                      wo_ref, o_ref, ubuf):
    sb_width = osb_ref.shape[2]
    o_pool = _pool_mix(u_ref, ubuf, wpool_ref, ps_ref, pl.program_id(1))
    y = x_ref[0] + jnp.dot(osb_ref[0], wout_ref[0:sb_width, :], preferred_element_type=F32)
    y = y + jnp.dot(o_pool, wout_ref[sb_width:, :], preferred_element_type=F32)
    o_ref[0] = y + _cross_attention(y, g_ref, wq_ref, kv_ref, wo_ref)


def _mix_xattn(x, o_sb, u, w_out, w_pool, pool_scale, g, w_q, kv, w_o, *, ts=512):
    b, s, d = x.shape
    m = kv.shape[1]
    sb_width = o_sb.shape[2]
    pool_width = u.shape[2]
    assert POOL_HALO >= max(POOL_WINDOWS) - 1 and ts >= POOL_HALO
    blk = lambda n: pl.BlockSpec((1, ts, n), lambda i, j: (i, j, 0))
    return pl.pallas_call(
        _mix_xattn_kernel,
        grid=(b, s // ts),
        in_specs=[blk(d), blk(sb_width), blk(pool_width), _resident(w_out.shape),
                  _resident(w_pool.shape), _resident((1, pool_width)), _resident((1, d)),
                  _resident((d, d)), pl.BlockSpec((1, m, 2 * d), lambda i, j: (i, 0, 0)),
                  _resident((d, d))],
        out_specs=blk(d),
        out_shape=jax.ShapeDtypeStruct((b, s, d), F32),
        scratch_shapes=[pltpu.VMEM((ts + POOL_HALO, pool_width), F32)],
        compiler_params=_params("parallel", "arbitrary"),
        name="mix_xattn",
    )(x, o_sb, u, w_out, w_pool, pool_scale, g, w_q, kv, w_o)


def _memkv_kernel(m_ref, g_ref, w_ref, o_ref):
    h = _rms(m_ref[...], g_ref[...]).astype(BF16)
    o_ref[...] = jnp.dot(h, w_ref[...], preferred_element_type=F32).astype(o_ref.dtype)


def _mem_kv(mem2d, g, w_kv, *, tm=1024):
    t, d = mem2d.shape
    n = w_kv.shape[1]
    return pl.pallas_call(
        _memkv_kernel,
        grid=(t // tm,),
        in_specs=[pl.BlockSpec((tm, d), lambda i: (i, 0)), _resident((1, d)), _resident((d, n))],
        out_specs=pl.BlockSpec((tm, n), lambda i: (i, 0)),
        out_shape=jax.ShapeDtypeStruct((t, n), BF16),
        compiler_params=_params("parallel"),
        name="mem_kv",
    )(mem2d, g, w_kv)


def kernel(x, mem, ffn1_norm, ffn1_w_gate, ffn1_w_up, ffn1_w_down, mix_norm, w_in, w_pool,
           pool_scale, w_out, mem_q_norm, mem_kv_norm, mem_w_q, mem_w_kv, mem_w_o,
           ffn2_norm, ffn2_w_gate, ffn2_w_up, ffn2_w_down, final_norm):
    b, s, d = x.shape
    m = mem.shape[1]
    depth = w_in.shape[0]
    sb_width = w_out.shape[1] // 2
    bf = lambda a: a.astype(BF16)
    vec = lambda a: a.reshape(1, -1)

    x2d = x.reshape(b * s, d)
    for l in range(depth):
        last = l == depth - 1
        x2d = _ffn(x2d, vec(ffn1_norm[l]), bf(ffn1_w_gate[l]), bf(ffn1_w_up[l]), bf(ffn1_w_down[l]))
        q, k, v, u = _proj(x2d, vec(mix_norm[l]), bf(w_in[l]), sb_width=sb_width)
        shp = lambda a: a.reshape(b, s, a.shape[-1])
        o_sb = _sb_attention(shp(q), shp(k), shp(v))
        kv = _mem_kv(mem.reshape(b * m, d), vec(mem_kv_norm[l]), bf(mem_w_kv[l]))
        x3d = _mix_xattn(x2d.reshape(b, s, d), o_sb, shp(u), bf(w_out[l]), bf(w_pool[l]),
                         vec(pool_scale[l]), vec(mem_q_norm[l]), bf(mem_w_q[l]),
                         kv.reshape(b, m, 2 * d), bf(mem_w_o[l]))
        x2d = _ffn(x3d.reshape(b * s, d), vec(ffn2_norm[l]), bf(ffn2_w_gate[l]), bf(ffn2_w_up[l]),
                   bf(ffn2_w_down[l]), vec(final_norm) if last else None)
    return x2d.reshape(b, s, d)
```

```python
import functools

import jax
import jax.numpy as jnp
from jax import lax
from jax.experimental import pallas as pl
from jax.experimental.pallas import tpu as pltpu

F32 = jnp.float32
BF16 = jnp.bfloat16

EPS = 1e-6
FFN_RESIDUAL_WEIGHT = 0.5
SB_HEAD_DIM = 64
SB_HEADS_PER_STEP = 2
POOL_WINDOWS = (2, 4, 8, 16)
POOL_HALO = 16
MEM_HEADS = 4

LANES = 128
SUBLANES = 8
SB_TQ = 256
SB_TK = 256
SB_STAGES = 4
SB_STREAMS = 2
VMEM_LIMIT = 60 * 1024 * 1024
LOG2_E = 1.4426950408889634
SB_MASKED = -1e30
SB_DEAD_CARRY = 150.0


def _rms(x, g):
    return x * lax.rsqrt(jnp.mean(x * x, axis=-1, keepdims=True) + EPS) * g


def _resident(shape):
    zeros = (0,) * len(shape)
    return pl.BlockSpec(shape, lambda *_: zeros, pipeline_mode=pl.Buffered(1))


def _params(*sem):
    return pltpu.CompilerParams(dimension_semantics=sem, vmem_limit_bytes=VMEM_LIMIT)


def _ffn_kernel(x_ref, g_ref, wg_ref, wu_ref, wd_ref, *rest, final_norm):
    o_ref = rest[-1]
    x = x_ref[...]
    h = _rms(x, g_ref[...]).astype(BF16)
    gate = jnp.dot(h, wg_ref[...], preferred_element_type=F32)
    up = jnp.dot(h, wu_ref[...], preferred_element_type=F32)
    act = (jax.nn.silu(gate) * up).astype(BF16)
    y = x + FFN_RESIDUAL_WEIGHT * jnp.dot(act, wd_ref[...], preferred_element_type=F32)
    if final_norm:
        y = _rms(y, rest[0][...])
    o_ref[...] = y


def _ffn(x2d, g, wg, wu, wd, gf=None, *, tm=512):
    t, d = x2d.shape
    dff = wg.shape[1]
    row = pl.BlockSpec((tm, d), lambda i: (i, 0))
    vec = _resident((1, d))
    in_specs = [row, vec, _resident((d, dff)), _resident((d, dff)), _resident((dff, d))]
    args = [x2d, g, wg, wu, wd]
    if gf is not None:
        in_specs.append(vec)
        args.append(gf)
    return pl.pallas_call(
        functools.partial(_ffn_kernel, final_norm=gf is not None),
        grid=(t // tm,),
        in_specs=in_specs,
        out_specs=row,
        out_shape=jax.ShapeDtypeStruct((t, d), F32),
        compiler_params=_params("parallel"),
        name="ffn_final" if gf is not None else "ffn",
    )(*args)


def _proj_kernel(x_ref, g_ref, w_ref, q_ref, k_ref, v_ref, u_ref, *, q_scale):
    h = _rms(x_ref[...], g_ref[...]).astype(BF16)
    p = jnp.dot(h, w_ref[...], preferred_element_type=F32)
    w = q_ref.shape[1]
    q_ref[...] = (p[:, :w] * q_scale).astype(BF16)
    k_ref[...] = p[:, w:2 * w].astype(BF16)
    v_ref[...] = p[:, 2 * w:3 * w].astype(BF16)
    u_ref[...] = p[:, 3 * w:]


def _proj(x2d, g, w_in, *, sb_width, tm=512):
    t, d = x2d.shape
    cols = w_in.shape[1]
    pool_width = cols - 3 * sb_width
    row = lambda n: pl.BlockSpec((tm, n), lambda i: (i, 0))
    return pl.pallas_call(
        functools.partial(_proj_kernel, q_scale=LOG2_E * SB_HEAD_DIM ** -0.5),
        grid=(t // tm,),
        in_specs=[row(d), _resident((1, d)), _resident((d, cols))],
        out_specs=[row(sb_width), row(sb_width), row(sb_width), row(pool_width)],
        out_shape=[jax.ShapeDtypeStruct((t, sb_width), BF16)] * 3
        + [jax.ShapeDtypeStruct((t, pool_width), F32)],
        compiler_params=_params("parallel"),
        name="proj",
    )(x2d, g, w_in)


def _sb_scores(z2, carry):
    pos = jnp.maximum(z2, 0.0)
    neg = z2 - pos
    soft = jnp.log2(1.0 + jnp.exp2(neg - pos))
    rest = pos + soft
    return rest.astype(BF16), neg - soft - carry, carry + jnp.sum(rest, axis=1, keepdims=True)


def _sb_kernel(q_ref, k_ref, v_ref, tri_ref, bias_ref, o_ref,
               z_ref, rest_ref, wgt_ref, lbc_ref, acc_ref, carry_ref):
    n_q = q_ref.shape[1] // SB_TQ
    heads = range(SB_HEADS_PER_STEP)
    lane = lax.broadcasted_iota(jnp.int32, (1, LANES), 1)
    head_lanes = [lane < SB_HEAD_DIM, lane >= SB_HEAD_DIM]
    nt = (((1,), (1,)), ((), ()))
    i32 = lambda b: b.astype(jnp.int32)

    for ref in (z_ref, rest_ref, wgt_ref, lbc_ref):
        ref[1] = jnp.zeros(ref.shape[1:], ref.dtype)
    for ref in (acc_ref, carry_ref):
        ref[...] = jnp.zeros(ref.shape, ref.dtype)

    def rows(block):
        return pl.ds(pl.multiple_of(block * SB_TK, SB_TK), SB_TK)

    def lanes(stream):
        return slice(stream * LANES, (stream + 1) * LANES)

    def step(cur, state):
        gens, (qi1, d1, last1), stage2, (qi3, d3, last3), drained = state
        prev = 1 - cur
        gen_qi, gen_d = gens[cur]
        real = gen_qi < n_q
        qi0 = jnp.where(real, gen_qi, n_q - 1)
        d0 = jnp.where(real, gen_d, 0)
        q = q_ref[0, rows(qi0), lanes(cur)]
        kb = k_ref[0, rows(qi0 - d0), lanes(cur)]
        vb = v_ref[0, rows(qi3 - d3), lanes(prev)]
        bias = bias_ref[jnp.minimum(d0, 1)]
        z_new = [lax.dot_general(jnp.where(head_lanes[h], q, jnp.zeros_like(q)), kb, nt,
                                 preferred_element_type=F32) + bias for h in heads]
        sufs = [jnp.dot(rest_ref[prev, h], tri_ref[...], preferred_element_type=F32) for h in heads]
        pvs = [jnp.dot(wgt_ref[prev, h], vb, preferred_element_type=F32) for h in heads]
        carries = []
        for h in heads:
            carry = jnp.where(d1 == 0, 0.0, carry_ref[prev, h])
            rest_ref[cur, h], lbc_ref[cur, h], carry = _sb_scores(z_ref[prev, h], carry)
            carry_ref[prev, h] = carry
            carries.append(carry)
        for h in heads:
            z_ref[cur, h] = z_new[h]
            wgt_ref[cur, h] = jnp.exp2(lbc_ref[prev, h] - sufs[h]).astype(BF16)
            acc_ref[prev, h] = jnp.where(d3 == 0, 0.0, acc_ref[prev, h]) + pvs[h]

        low = jnp.min(jnp.minimum(*carries).reshape(SB_TQ // SUBLANES, SUBLANES, 1), axis=0)
        for shift in (4, 2, 1):
            low = jnp.minimum(low, pltpu.roll(low, shift, axis=0))
        dead = low[0, 0] >= SB_DEAD_CARRY

        @pl.when(last3 == 1)
        def _():
            o_ref[0, rows(qi3), lanes(prev)] = jnp.where(
                head_lanes[0], acc_ref[prev, 0], acc_ref[prev, 1]).astype(o_ref.dtype)

        last0 = real & (d0 == qi0)
        gen_cur = (jnp.where(last0, gen_qi + 1, gen_qi), jnp.where(last0, 0, gen_d + i32(real)))
        oth_qi, oth_d = gens[prev]
        cut = dead & (oth_qi == qi1)
        gen_oth = (jnp.where(cut, oth_qi + 1, oth_qi), jnp.where(cut, 0, oth_d))
        new_gens = (gen_cur, gen_oth) if cur == 0 else (gen_oth, gen_cur)
        both_out = (new_gens[0][0] >= n_q) & (new_gens[1][0] >= n_q)
        return (new_gens, (qi0, d0, i32(last0)), (qi1, d1, last1 | i32(cut)), stage2,
                jnp.where(both_out, drained + 1, 0))

    def step_pair(state):
        return step(1, step(0, state))

    zero = jnp.int32(0)
    idle = (jnp.int32(n_q - 1), zero, zero)
    state = (((zero, zero), (zero, zero)), idle, idle, idle, zero)
    lax.while_loop(lambda st: st[-1] < SB_STAGES, step_pair, state)


def _sb_attention(q, k, v):
    b, s, w = q.shape
    assert SB_TQ == SB_TK and s % SB_TQ == 0 and w % (SB_STREAMS * LANES) == 0
    assert SB_HEADS_PER_STEP * SB_HEAD_DIM == LANES
    assert s // SB_TQ > 1
    row = lax.broadcasted_iota(jnp.int32, (SB_TQ, SB_TK), 0)
    col = lax.broadcasted_iota(jnp.int32, (SB_TQ, SB_TK), 1)
    tri = (row > col).astype(BF16)
    diag_bias = jnp.where(col < row, 0.0, SB_MASKED).astype(F32)
    bias = jnp.stack([diag_bias, jnp.zeros_like(diag_bias)])
    blk = pl.BlockSpec((1, s, SB_STREAMS * LANES), lambda i, j: (i, 0, j))
    per_stream = lambda cols, dtype: pltpu.VMEM((SB_STREAMS, SB_HEADS_PER_STEP, SB_TQ, cols), dtype)
    per_parity = lambda dtype: pltpu.VMEM((2, SB_HEADS_PER_STEP, SB_TQ, SB_TK), dtype)
    return pl.pallas_call(
        _sb_kernel,
        grid=(b, w // (SB_STREAMS * LANES)),
        in_specs=[blk, blk, blk, _resident(tri.shape), _resident(bias.shape)],
        out_specs=blk,
        out_shape=jax.ShapeDtypeStruct((b, s, w), BF16),
        scratch_shapes=[per_parity(F32), per_parity(BF16), per_parity(BF16), per_parity(F32),
                        per_stream(LANES, F32), per_stream(1, F32)],
        compiler_params=_params("parallel", "parallel"),
        name="sb_attn",
    )(q, k, v, tri, bias)


def _pool_mix(u_ref, ubuf, wpool_ref, ps_ref, si):
    ts = u_ref.shape[1]
    gdim = wpool_ref.shape[1]

    @pl.when(si == 0)
    def _():
        ubuf[0:POOL_HALO, :] = jnp.zeros((POOL_HALO, ubuf.shape[1]), F32)

    ubuf[POOL_HALO:, :] = u_ref[0]
    pos = si * ts + lax.broadcasted_iota(jnp.int32, (ts, 1), 0)
    pooled_out = []
    for g, w in enumerate(POOL_WINDOWS):
        cols = slice(g * gdim, (g + 1) * gdim)
        total = ubuf[:, cols]
        shift = 1
        while shift < w:
            total = total + pltpu.roll(total, shift, axis=0)
            shift *= 2
        token = ubuf[POOL_HALO:, cols]
        inv_count = 1.0 / jnp.minimum(pos + 1, w).astype(F32)
        pooled = (total[POOL_HALO:] * inv_count - token).astype(BF16)
        pooled_out.append(jnp.dot(pooled, wpool_ref[g], preferred_element_type=F32))
    o_pool = (jnp.concatenate(pooled_out, axis=1) * ps_ref[...]).astype(BF16)
    ubuf[0:POOL_HALO, :] = ubuf[ts:ts + POOL_HALO, :]
    return o_pool


def _cross_attention(x, g_ref, wq_ref, kv_ref, wo_ref):
    d = x.shape[1]
    hd = d // MEM_HEADS
    h = _rms(x, g_ref[...]).astype(BF16)
    q = (jnp.dot(h, wq_ref[...], preferred_element_type=F32) * hd ** -0.5).astype(BF16)
    heads = []
    for i in range(MEM_HEADS):
        qh = q[:, i * hd:(i + 1) * hd]
        kh = kv_ref[0, :, i * hd:(i + 1) * hd]
        vh = kv_ref[0, :, d + i * hd:d + (i + 1) * hd]
        s = lax.dot_general(qh, kh, (((1,), (1,)), ((), ())), preferred_element_type=F32)
        p = jnp.exp(s - jnp.max(s, axis=-1, keepdims=True))
        inv = 1.0 / jnp.sum(p, axis=-1, keepdims=True)
        heads.append((jnp.dot(p.astype(BF16), vh, preferred_element_type=F32) * inv).astype(BF16))
    o = jnp.concatenate(heads, axis=1)
    return jnp.dot(o, wo_ref[...], preferred_element_type=F32)


def _mix_xattn_kernel(x_ref, osb_ref, u_ref, wout_ref, wpool_ref, ps_ref, g_ref, wq_ref, kv_ref,
                      wo_ref, o_ref, ubuf):
    sb_width = osb_ref.shape[2]
    o_pool = _pool_mix(u_ref, ubuf, wpool_ref, ps_ref, pl.program_id(1))
    y = x_ref[0] + jnp.dot(osb_ref[0], wout_ref[0:sb_width, :], preferred_element_type=F32)
    y = y + jnp.dot(o_pool, wout_ref[sb_width:, :], preferred_element_type=F32)
    o_ref[0] = y + _cross_attention(y, g_ref, wq_ref, kv_ref, wo_ref)


def _mix_xattn(x, o_sb, u, w_out, w_pool, pool_scale, g, w_q, kv, w_o, *, ts=512):
    b, s, d = x.shape
    m = kv.shape[1]
    sb_width = o_sb.shape[2]
    pool_width = u.shape[2]
    assert POOL_HALO >= max(POOL_WINDOWS) - 1 and ts >= POOL_HALO
    blk = lambda n: pl.BlockSpec((1, ts, n), lambda i, j: (i, j, 0))
    return pl.pallas_call(
        _mix_xattn_kernel,
        grid=(b, s // ts),
        in_specs=[blk(d), blk(sb_width), blk(pool_width), _resident(w_out.shape),
                  _resident(w_pool.shape), _resident((1, pool_width)), _resident((1, d)),
                  _resident((d, d)), pl.BlockSpec((1, m, 2 * d), lambda i, j: (i, 0, 0)),
                  _resident((d, d))],
        out_specs=blk(d),
        out_shape=jax.ShapeDtypeStruct((b, s, d), F32),
        scratch_shapes=[pltpu.VMEM((ts + POOL_HALO, pool_width), F32)],
        compiler_params=_params("parallel", "arbitrary"),
        name="mix_xattn",
    )(x, o_sb, u, w_out, w_pool, pool_scale, g, w_q, kv, w_o)


def _memkv_kernel(m_ref, g_ref, w_ref, o_ref):
    h = _rms(m_ref[...], g_ref[...]).astype(BF16)
    o_ref[...] = jnp.dot(h, w_ref[...], preferred_element_type=F32).astype(o_ref.dtype)


def _mem_kv(mem2d, g, w_kv, *, tm=1024):
    t, d = mem2d.shape
    n = w_kv.shape[1]
    return pl.pallas_call(
        _memkv_kernel,
        grid=(t // tm,),
        in_specs=[pl.BlockSpec((tm, d), lambda i: (i, 0)), _resident((1, d)), _resident((d, n))],
        out_specs=pl.BlockSpec((tm, n), lambda i: (i, 0)),
        out_shape=jax.ShapeDtypeStruct((t, n), BF16),
        compiler_params=_params("parallel"),
        name="mem_kv",
    )(mem2d, g, w_kv)


def kernel(x, mem, ffn1_norm, ffn1_w_gate, ffn1_w_up, ffn1_w_down, mix_norm, w_in, w_pool,
           pool_scale, w_out, mem_q_norm, mem_kv_norm, mem_w_q, mem_w_kv, mem_w_o,
           ffn2_norm, ffn2_w_gate, ffn2_w_up, ffn2_w_down, final_norm):
    b, s, d = x.shape
    m = mem.shape[1]
    depth = w_in.shape[0]
    sb_width = w_out.shape[1] // 2
    bf = lambda a: a.astype(BF16)
    vec = lambda a: a.reshape(1, -1)

    x2d = x.reshape(b * s, d)
    for l in range(depth):
        last = l == depth - 1
        x2d = _ffn(x2d, vec(ffn1_norm[l]), bf(ffn1_w_gate[l]), bf(ffn1_w_up[l]), bf(ffn1_w_down[l]))
        q, k, v, u = _proj(x2d, vec(mix_norm[l]), bf(w_in[l]), sb_width=sb_width)
        shp = lambda a: a.reshape(b, s, a.shape[-1])
        o_sb = _sb_attention(shp(q), shp(k), shp(v))
        kv = _mem_kv(mem.reshape(b * m, d), vec(mem_kv_norm[l]), bf(mem_w_kv[l]))
        x3d = _mix_xattn(x2d.reshape(b, s, d), o_sb, shp(u), bf(w_out[l]), bf(w_pool[l]),
                         vec(pool_scale[l]), vec(mem_q_norm[l]), bf(mem_w_q[l]),
                         kv.reshape(b, m, 2 * d), bf(mem_w_o[l]))
        x2d = _ffn(x3d.reshape(b * s, d), vec(ffn2_norm[l]), bf(ffn2_w_gate[l]), bf(ffn2_w_up[l]),
                   bf(ffn2_w_down[l]), vec(final_norm) if last else None)
    return x2d.reshape(b, s, d)
```

```python
import functools

import jax
import jax.numpy as jnp
from jax import lax
from jax.experimental import pallas as pl
from jax.experimental.pallas import tpu as pltpu

F32 = jnp.float32
BF16 = jnp.bfloat16

EPS = 1e-6
FFN_RESIDUAL_WEIGHT = 0.5
SB_HEAD_DIM = 64
SB_HEADS_PER_STEP = 2
POOL_WINDOWS = (2, 4, 8, 16)
POOL_HALO = 16
MEM_HEADS = 4

LANES = 128
SUBLANES = 8
SB_TQ = 256
SB_TK = 256
SB_STAGES = 4
SB_STREAMS = 2
SB_BATCH = 4
VMEM_LIMIT = 60 * 1024 * 1024
LOG2_E = 1.4426950408889634
SB_MASKED = -1e30
SB_DEAD_CARRY = 150.0


def _rms(x, g):
    return x * lax.rsqrt(jnp.mean(x * x, axis=-1, keepdims=True) + EPS) * g


def _resident(shape):
    zeros = (0,) * len(shape)
    return pl.BlockSpec(shape, lambda *_: zeros, pipeline_mode=pl.Buffered(1))


def _params(*sem):
    return pltpu.CompilerParams(dimension_semantics=sem, vmem_limit_bytes=VMEM_LIMIT)


def _ffn_kernel(x_ref, g_ref, wg_ref, wu_ref, wd_ref, *rest, final_norm):
    o_ref = rest[-1]
    x = x_ref[...]
    h = _rms(x, g_ref[...]).astype(BF16)
    gate = jnp.dot(h, wg_ref[...], preferred_element_type=F32)
    up = jnp.dot(h, wu_ref[...], preferred_element_type=F32)
    act = (jax.nn.silu(gate) * up).astype(BF16)
    y = x + FFN_RESIDUAL_WEIGHT * jnp.dot(act, wd_ref[...], preferred_element_type=F32)
    if final_norm:
        y = _rms(y, rest[0][...])
    o_ref[...] = y


def _ffn(x2d, g, wg, wu, wd, gf=None, *, tm=512):
    t, d = x2d.shape
    dff = wg.shape[1]
    row = pl.BlockSpec((tm, d), lambda i: (i, 0))
    vec = _resident((1, d))
    in_specs = [row, vec, _resident((d, dff)), _resident((d, dff)), _resident((dff, d))]
    args = [x2d, g, wg, wu, wd]
    if gf is not None:
        in_specs.append(vec)
        args.append(gf)
    return pl.pallas_call(
        functools.partial(_ffn_kernel, final_norm=gf is not None),
        grid=(t // tm,),
        in_specs=in_specs,
        out_specs=row,
        out_shape=jax.ShapeDtypeStruct((t, d), F32),
        compiler_params=_params("parallel"),
        name="ffn_final" if gf is not None else "ffn",
    )(*args)


def _proj_kernel(x_ref, g_ref, w_ref, q_ref, k_ref, v_ref, u_ref, *, q_scale):
    h = _rms(x_ref[...], g_ref[...]).astype(BF16)
    p = jnp.dot(h, w_ref[...], preferred_element_type=F32)
    w = q_ref.shape[1]
    q_ref[...] = (p[:, :w] * q_scale).astype(BF16)
    k_ref[...] = p[:, w:2 * w].astype(BF16)
    v_ref[...] = p[:, 2 * w:3 * w].astype(BF16)
    u_ref[...] = p[:, 3 * w:]


def _proj(x2d, g, w_in, *, sb_width, tm=512):
    t, d = x2d.shape
    cols = w_in.shape[1]
    pool_width = cols - 3 * sb_width
    row = lambda n: pl.BlockSpec((tm, n), lambda i: (i, 0))
    return pl.pallas_call(
        functools.partial(_proj_kernel, q_scale=LOG2_E * SB_HEAD_DIM ** -0.5),
        grid=(t // tm,),
        in_specs=[row(d), _resident((1, d)), _resident((d, cols))],
        out_specs=[row(sb_width), row(sb_width), row(sb_width), row(pool_width)],
        out_shape=[jax.ShapeDtypeStruct((t, sb_width), BF16)] * 3
        + [jax.ShapeDtypeStruct((t, pool_width), F32)],
        compiler_params=_params("parallel"),
        name="proj",
    )(x2d, g, w_in)


def _sb_scores(z2, carry):
    pos = jnp.maximum(z2, 0.0)
    neg = z2 - pos
    soft = jnp.log2(1.0 + jnp.exp2(neg - pos))
    rest = pos + soft
    return rest.astype(BF16), neg - soft - carry, carry + jnp.sum(rest, axis=1, keepdims=True)


def _sb_kernel(q_ref, k_ref, v_ref, tri_ref, bias_ref, o_ref,
               z_ref, rest_ref, wgt_ref, lbc_ref, acc_ref, carry_ref):
    n_q = q_ref.shape[1] // SB_TQ
    n_blk = q_ref.shape[0] * n_q
    heads = range(SB_HEADS_PER_STEP)
    lane = lax.broadcasted_iota(jnp.int32, (1, LANES), 1)
    head_lanes = [lane < SB_HEAD_DIM, lane >= SB_HEAD_DIM]
    nt = (((1,), (1,)), ((), ()))
    i32 = lambda b: b.astype(jnp.int32)

    for ref in (z_ref, rest_ref, wgt_ref, lbc_ref):
        ref[1] = jnp.zeros(ref.shape[1:], ref.dtype)
    for ref in (acc_ref, carry_ref):
        ref[...] = jnp.zeros(ref.shape, ref.dtype)

    def rows(block):
        return pl.ds(pl.multiple_of(block * SB_TK, SB_TK), SB_TK)

    def lanes(stream):
        return slice(stream * LANES, (stream + 1) * LANES)

    def step(cur, state):
        gens, (blk1, d1, last1), stage2, (blk3, d3, last3), drained = state
        prev = 1 - cur
        gen_blk, gen_d = gens[cur]
        real = gen_blk < n_blk
        blk0 = jnp.where(real, gen_blk, n_blk - 1)
        d0 = jnp.where(real, gen_d, 0)
        b0, qi0 = blk0 // n_q, blk0 % n_q
        b3, qi3 = blk3 // n_q, blk3 % n_q
        q = q_ref[b0, rows(qi0), lanes(cur)]
        kb = k_ref[b0, rows(qi0 - d0), lanes(cur)]
        vb = v_ref[b3, rows(qi3 - d3), lanes(prev)]
        bias = bias_ref[jnp.minimum(d0, 1)]
        z_new = [lax.dot_general(jnp.where(head_lanes[h], q, jnp.zeros_like(q)), kb, nt,
                                 preferred_element_type=F32) + bias for h in heads]
        sufs = [jnp.dot(rest_ref[prev, h], tri_ref[...], preferred_element_type=F32) for h in heads]
        pvs = [jnp.dot(wgt_ref[prev, h], vb, preferred_element_type=F32) for h in heads]
        carries = []
        for h in heads:
            carry = jnp.where(d1 == 0, 0.0, carry_ref[prev, h])
            rest_ref[cur, h], lbc_ref[cur, h], carry = _sb_scores(z_ref[prev, h], carry)
            carry_ref[prev, h] = carry
            carries.append(carry)
        for h in heads:
            z_ref[cur, h] = z_new[h]
            wgt_ref[cur, h] = jnp.exp2(lbc_ref[prev, h] - sufs[h]).astype(BF16)
            acc_ref[prev, h] = jnp.where(d3 == 0, 0.0, acc_ref[prev, h]) + pvs[h]

        low = jnp.min(jnp.minimum(*carries).reshape(SB_TQ // SUBLANES, SUBLANES, 1), axis=0)
        for shift in (4, 2, 1):
            low = jnp.minimum(low, pltpu.roll(low, shift, axis=0))
        dead = low[0, 0] >= SB_DEAD_CARRY

        @pl.when(last3 == 1)
        def _():
            o_ref[b3, rows(qi3), lanes(prev)] = jnp.where(
                head_lanes[0], acc_ref[prev, 0], acc_ref[prev, 1]).astype(o_ref.dtype)

        last0 = real & (d0 == qi0)
        gen_cur = (jnp.where(last0, gen_blk + 1, gen_blk), jnp.where(last0, 0, gen_d + i32(real)))
        oth_blk, oth_d = gens[prev]
        cut = dead & (oth_blk == blk1)
        gen_oth = (jnp.where(cut, oth_blk + 1, oth_blk), jnp.where(cut, 0, oth_d))
        new_gens = (gen_cur, gen_oth) if cur == 0 else (gen_oth, gen_cur)
        both_out = (new_gens[0][0] >= n_blk) & (new_gens[1][0] >= n_blk)
        return (new_gens, (blk0, d0, i32(last0)), (blk1, d1, last1 | i32(cut)), stage2,
                jnp.where(both_out, drained + 1, 0))

    def step_pair(state):
        return step(1, step(0, state))

    zero = jnp.int32(0)
    idle = (jnp.int32(n_blk - 1), zero, zero)
    state = (((zero, zero), (zero, zero)), idle, idle, idle, zero)
    lax.while_loop(lambda st: st[-1] < SB_STAGES, step_pair, state)


def _sb_attention(q, k, v):
    b, s, w = q.shape
    assert SB_TQ == SB_TK and s % SB_TQ == 0 and w % (SB_STREAMS * LANES) == 0 and b % SB_BATCH == 0
    assert SB_HEADS_PER_STEP * SB_HEAD_DIM == LANES
    assert s // SB_TQ > 1
    row = lax.broadcasted_iota(jnp.int32, (SB_TQ, SB_TK), 0)
    col = lax.broadcasted_iota(jnp.int32, (SB_TQ, SB_TK), 1)
    tri = (row > col).astype(BF16)
    diag_bias = jnp.where(col < row, 0.0, SB_MASKED).astype(F32)
    bias = jnp.stack([diag_bias, jnp.zeros_like(diag_bias)])
    blk = pl.BlockSpec((SB_BATCH, s, SB_STREAMS * LANES), lambda i, j: (i, 0, j))
    per_stream = lambda cols, dtype: pltpu.VMEM((SB_STREAMS, SB_HEADS_PER_STEP, SB_TQ, cols), dtype)
    per_parity = lambda dtype: pltpu.VMEM((2, SB_HEADS_PER_STEP, SB_TQ, SB_TK), dtype)
    return pl.pallas_call(
        _sb_kernel,
        grid=(b // SB_BATCH, w // (SB_STREAMS * LANES)),
        in_specs=[blk, blk, blk, _resident(tri.shape), _resident(bias.shape)],
        out_specs=blk,
        out_shape=jax.ShapeDtypeStruct((b, s, w), BF16),
        scratch_shapes=[per_parity(F32), per_parity(BF16), per_parity(BF16), per_parity(F32),
                        per_stream(LANES, F32), per_stream(1, F32)],
        compiler_params=_params("parallel", "parallel"),
        name="sb_attn",
    )(q, k, v, tri, bias)


def _pool_mix(u_ref, ubuf, wpool_ref, ps_ref, si):
    ts = u_ref.shape[1]
    gdim = wpool_ref.shape[1]

    @pl.when(si == 0)
    def _():
        ubuf[0:POOL_HALO, :] = jnp.zeros((POOL_HALO, ubuf.shape[1]), F32)

    ubuf[POOL_HALO:, :] = u_ref[0]
    pos = si * ts + lax.broadcasted_iota(jnp.int32, (ts, 1), 0)
    pooled_out = []
    for g, w in enumerate(POOL_WINDOWS):
        cols = slice(g * gdim, (g + 1) * gdim)
        total = ubuf[:, cols]
        shift = 1
        while shift < w:
            total = total + pltpu.roll(total, shift, axis=0)
            shift *= 2
        token = ubuf[POOL_HALO:, cols]
        inv_count = 1.0 / jnp.minimum(pos + 1, w).astype(F32)
        pooled = (total[POOL_HALO:] * inv_count - token).astype(BF16)
        pooled_out.append(jnp.dot(pooled, wpool_ref[g], preferred_element_type=F32))
    o_pool = (jnp.concatenate(pooled_out, axis=1) * ps_ref[...]).astype(BF16)
    ubuf[0:POOL_HALO, :] = ubuf[ts:ts + POOL_HALO, :]
    return o_pool


def _cross_attention(x, g_ref, wq_ref, kv_ref, wo_ref):
    d = x.shape[1]
    hd = d // MEM_HEADS
    h = _rms(x, g_ref[...]).astype(BF16)
    q = (jnp.dot(h, wq_ref[...], preferred_element_type=F32) * hd ** -0.5).astype(BF16)
    heads = []
    for i in range(MEM_HEADS):
        qh = q[:, i * hd:(i + 1) * hd]
        kh = kv_ref[0, :, i * hd:(i + 1) * hd]
        vh = kv_ref[0, :, d + i * hd:d + (i + 1) * hd]
        s = lax.dot_general(qh, kh, (((1,), (1,)), ((), ())), preferred_element_type=F32)
        p = jnp.exp(s - jnp.max(s, axis=-1, keepdims=True))
        inv = 1.0 / jnp.sum(p, axis=-1, keepdims=True)
        heads.append((jnp.dot(p.astype(BF16), vh, preferred_element_type=F32) * inv).astype(BF16))
    o = jnp.concatenate(heads, axis=1)
    return jnp.dot(o, wo_ref[...], preferred_element_type=F32)


def _mix_xattn_kernel(x_ref, osb_ref, u_ref, wout_ref, wpool_ref, ps_ref, g_ref, wq_ref, kv_ref,
                      wo_ref, o_ref, ubuf):
    sb_width = osb_ref.shape[2]
    o_pool = _pool_mix(u_ref, ubuf, wpool_ref, ps_ref, pl.program_id(1))
    y = x_ref[0] + jnp.dot(osb_ref[0], wout_ref[0:sb_width, :], preferred_element_type=F32)
    y = y + jnp.dot(o_pool, wout_ref[sb_width:, :], preferred_element_type=F32)
    o_ref[0] = y + _cross_attention(y, g_ref, wq_ref, kv_ref, wo_ref)


def _mix_xattn(x, o_sb, u, w_out, w_pool, pool_scale, g, w_q, kv, w_o, *, ts=512):
    b, s, d = x.shape
    m = kv.shape[1]
    sb_width = o_sb.shape[2]
    pool_width = u.shape[2]
    assert POOL_HALO >= max(POOL_WINDOWS) - 1 and ts >= POOL_HALO
    blk = lambda n: pl.BlockSpec((1, ts, n), lambda i, j: (i, j, 0))
    return pl.pallas_call(
        _mix_xattn_kernel,
        grid=(b, s // ts),
        in_specs=[blk(d), blk(sb_width), blk(pool_width), _resident(w_out.shape),
                  _resident(w_pool.shape), _resident((1, pool_width)), _resident((1, d)),
                  _resident((d, d)), pl.BlockSpec((1, m, 2 * d), lambda i, j: (i, 0, 0)),
                  _resident((d, d))],
        out_specs=blk(d),
        out_shape=jax.ShapeDtypeStruct((b, s, d), F32),
        scratch_shapes=[pltpu.VMEM((ts + POOL_HALO, pool_width), F32)],
        compiler_params=_params("parallel", "arbitrary"),
        name="mix_xattn",
    )(x, o_sb, u, w_out, w_pool, pool_scale, g, w_q, kv, w_o)


def _memkv_kernel(m_ref, g_ref, w_ref, o_ref):
    h = _rms(m_ref[...], g_ref[...]).astype(BF16)
    o_ref[...] = jnp.dot(h, w_ref[...], preferred_element_type=F32).astype(o_ref.dtype)


def _mem_kv(mem2d, g, w_kv, *, tm=1024):
    t, d = mem2d.shape
    n = w_kv.shape[1]
    return pl.pallas_call(
        _memkv_kernel,
        grid=(t // tm,),
        in_specs=[pl.BlockSpec((tm, d), lambda i: (i, 0)), _resident((1, d)), _resident((d, n))],
        out_specs=pl.BlockSpec((tm, n), lambda i: (i, 0)),
        out_shape=jax.ShapeDtypeStruct((t, n), BF16),
        compiler_params=_params("parallel"),
        name="mem_kv",
    )(mem2d, g, w_kv)


def kernel(x, mem, ffn1_norm, ffn1_w_gate, ffn1_w_up, ffn1_w_down, mix_norm, w_in, w_pool,
           pool_scale, w_out, mem_q_norm, mem_kv_norm, mem_w_q, mem_w_kv, mem_w_o,
           ffn2_norm, ffn2_w_gate, ffn2_w_up, ffn2_w_down, final_norm):
    b, s, d = x.shape
    m = mem.shape[1]
    depth = w_in.shape[0]
    sb_width = w_out.shape[1] // 2
    bf = lambda a: a.astype(BF16)
    vec = lambda a: a.reshape(1, -1)

    x2d = x.reshape(b * s, d)
    for l in range(depth):
        last = l == depth - 1
        x2d = _ffn(x2d, vec(ffn1_norm[l]), bf(ffn1_w_gate[l]), bf(ffn1_w_up[l]), bf(ffn1_w_down[l]))
        q, k, v, u = _proj(x2d, vec(mix_norm[l]), bf(w_in[l]), sb_width=sb_width)
        shp = lambda a: a.reshape(b, s, a.shape[-1])
        o_sb = _sb_attention(shp(q), shp(k), shp(v))
        kv = _mem_kv(mem.reshape(b * m, d), vec(mem_kv_norm[l]), bf(mem_w_kv[l]))
        x3d = _mix_xattn(x2d.reshape(b, s, d), o_sb, shp(u), bf(w_out[l]), bf(w_pool[l]),
                         vec(pool_scale[l]), vec(mem_q_norm[l]), bf(mem_w_q[l]),
                         kv.reshape(b, m, 2 * d), bf(mem_w_o[l]))
        x2d = _ffn(x3d.reshape(b * s, d), vec(ffn2_norm[l]), bf(ffn2_w_gate[l]), bf(ffn2_w_up[l]),
                   bf(ffn2_w_down[l]), vec(final_norm) if last else None)
    return x2d.reshape(b, s, d)
```

```python
import functools

import jax
import jax.numpy as jnp
from jax import lax
from jax.experimental import pallas as pl
from jax.experimental.pallas import tpu as pltpu

F32 = jnp.float32
BF16 = jnp.bfloat16

EPS = 1e-6
FFN_RESIDUAL_WEIGHT = 0.5
FFN_CHUNK = 1536
SB_HEAD_DIM = 64
SB_HEADS_PER_STEP = 2
POOL_WINDOWS = (2, 4, 8, 16)
POOL_HALO = 16
MEM_HEADS = 4

LANES = 128
SUBLANES = 8
SB_TQ = 256
SB_TK = 256
SB_STAGES = 4
SB_STREAMS = 2
SB_BATCH = 4
VMEM_LIMIT = 60 * 1024 * 1024
LOG2_E = 1.4426950408889634
SB_MASKED = -1e30
SB_DEAD_CARRY = 150.0


def _rms(x, g):
    return x * lax.rsqrt(jnp.mean(x * x, axis=-1, keepdims=True) + EPS) * g


def _resident(shape):
    zeros = (0,) * len(shape)
    return pl.BlockSpec(shape, lambda *_: zeros, pipeline_mode=pl.Buffered(1))


def _params(*sem):
    return pltpu.CompilerParams(dimension_semantics=sem, vmem_limit_bytes=VMEM_LIMIT)


def _ffn_kernel(x_ref, g_ref, wg_ref, wu_ref, wd_ref, *rest, final_norm):
    o_ref = rest[-1]
    x = x_ref[...]
    h = _rms(x, g_ref[...]).astype(BF16)
    dff = wg_ref.shape[1]
    out = None
    for c0 in range(0, dff, FFN_CHUNK):
        c1 = min(c0 + FFN_CHUNK, dff)
        gate = jnp.dot(h, wg_ref[:, c0:c1], preferred_element_type=F32)
        up = jnp.dot(h, wu_ref[:, c0:c1], preferred_element_type=F32)
        act = (jax.nn.silu(gate) * up).astype(BF16)
        part = jnp.dot(act, wd_ref[c0:c1, :], preferred_element_type=F32)
        out = part if out is None else out + part
    y = x + FFN_RESIDUAL_WEIGHT * out
    if final_norm:
        y = _rms(y, rest[0][...])
    o_ref[...] = y


def _ffn(x2d, g, wg, wu, wd, gf=None, *, tm=1024):
    t, d = x2d.shape
    dff = wg.shape[1]
    row = pl.BlockSpec((tm, d), lambda i: (i, 0))
    vec = _resident((1, d))
    in_specs = [row, vec, _resident((d, dff)), _resident((d, dff)), _resident((dff, d))]
    args = [x2d, g, wg, wu, wd]
    if gf is not None:
        in_specs.append(vec)
        args.append(gf)
    return pl.pallas_call(
        functools.partial(_ffn_kernel, final_norm=gf is not None),
        grid=(t // tm,),
        in_specs=in_specs,
        out_specs=row,
        out_shape=jax.ShapeDtypeStruct((t, d), F32),
        compiler_params=_params("parallel"),
        name="ffn_final" if gf is not None else "ffn",
    )(*args)


def _proj_kernel(x_ref, g_ref, w_ref, q_ref, k_ref, v_ref, u_ref, *, q_scale):
    h = _rms(x_ref[...], g_ref[...]).astype(BF16)
    p = jnp.dot(h, w_ref[...], preferred_element_type=F32)
    w = q_ref.shape[1]
    q_ref[...] = (p[:, :w] * q_scale).astype(BF16)
    k_ref[...] = p[:, w:2 * w].astype(BF16)
    v_ref[...] = p[:, 2 * w:3 * w].astype(BF16)
    u_ref[...] = p[:, 3 * w:]


def _proj(x2d, g, w_in, *, sb_width, tm=1024):
    t, d = x2d.shape
    cols = w_in.shape[1]
    pool_width = cols - 3 * sb_width
    row = lambda n: pl.BlockSpec((tm, n), lambda i: (i, 0))
    return pl.pallas_call(
        functools.partial(_proj_kernel, q_scale=LOG2_E * SB_HEAD_DIM ** -0.5),
        grid=(t // tm,),
        in_specs=[row(d), _resident((1, d)), _resident((d, cols))],
        out_specs=[row(sb_width), row(sb_width), row(sb_width), row(pool_width)],
        out_shape=[jax.ShapeDtypeStruct((t, sb_width), BF16)] * 3
        + [jax.ShapeDtypeStruct((t, pool_width), F32)],
        compiler_params=_params("parallel"),
        name="proj",
    )(x2d, g, w_in)


def _sb_scores(z2, carry):
    pos = jnp.maximum(z2, 0.0)
    neg = z2 - pos
    soft = jnp.log2(1.0 + jnp.exp2(neg - pos))
    rest = pos + soft
    return rest.astype(BF16), neg - soft - carry, carry + jnp.sum(rest, axis=1, keepdims=True)


def _sb_kernel(q_ref, k_ref, v_ref, tri_ref, bias_ref, o_ref,
               z_ref, rest_ref, wgt_ref, lbc_ref, acc_ref, carry_ref):
    n_q = q_ref.shape[1] // SB_TQ
    n_blk = q_ref.shape[0] * n_q
    heads = range(SB_HEADS_PER_STEP)
    lane = lax.broadcasted_iota(jnp.int32, (1, LANES), 1)
    head_lanes = [lane < SB_HEAD_DIM, lane >= SB_HEAD_DIM]
    nt = (((1,), (1,)), ((), ()))
    i32 = lambda b: b.astype(jnp.int32)

    for ref in (z_ref, rest_ref, wgt_ref, lbc_ref):
        ref[1] = jnp.zeros(ref.shape[1:], ref.dtype)
    for ref in (acc_ref, carry_ref):
        ref[...] = jnp.zeros(ref.shape, ref.dtype)

    def rows(block):
        return pl.ds(pl.multiple_of(block * SB_TK, SB_TK), SB_TK)

    def lanes(stream):
        return slice(stream * LANES, (stream + 1) * LANES)

    def step(cur, state):
        gens, (blk1, d1, last1), stage2, (blk3, d3, last3), drained = state
        prev = 1 - cur
        gen_blk, gen_d = gens[cur]
        real = gen_blk < n_blk
        blk0 = jnp.where(real, gen_blk, n_blk - 1)
        d0 = jnp.where(real, gen_d, 0)
        b0, qi0 = blk0 // n_q, blk0 % n_q
        b3, qi3 = blk3 // n_q, blk3 % n_q
        q = q_ref[b0, rows(qi0), lanes(cur)]
        kb = k_ref[b0, rows(qi0 - d0), lanes(cur)]
        vb = v_ref[b3, rows(qi3 - d3), lanes(prev)]
        bias = bias_ref[jnp.minimum(d0, 1)]
        z_new = [lax.dot_general(jnp.where(head_lanes[h], q, jnp.zeros_like(q)), kb, nt,
                                 preferred_element_type=F32) + bias for h in heads]
        sufs = [jnp.dot(rest_ref[prev, h], tri_ref[...], preferred_element_type=F32) for h in heads]
        pvs = [jnp.dot(wgt_ref[prev, h], vb, preferred_element_type=F32) for h in heads]
        carries = []
        for h in heads:
            carry = jnp.where(d1 == 0, 0.0, carry_ref[prev, h])
            rest_ref[cur, h], lbc_ref[cur, h], carry = _sb_scores(z_ref[prev, h], carry)
            carry_ref[prev, h] = carry
            carries.append(carry)
        for h in heads:
            z_ref[cur, h] = z_new[h]
            wgt_ref[cur, h] = jnp.exp2(lbc_ref[prev, h] - sufs[h]).astype(BF16)
            acc_ref[prev, h] = jnp.where(d3 == 0, 0.0, acc_ref[prev, h]) + pvs[h]

        low = jnp.min(jnp.minimum(*carries).reshape(SB_TQ // SUBLANES, SUBLANES, 1), axis=0)
        for shift in (4, 2, 1):
            low = jnp.minimum(low, pltpu.roll(low, shift, axis=0))
        dead = low[0, 0] >= SB_DEAD_CARRY

        @pl.when(last3 == 1)
        def _():
            o_ref[b3, rows(qi3), lanes(prev)] = jnp.where(
                head_lanes[0], acc_ref[prev, 0], acc_ref[prev, 1]).astype(o_ref.dtype)

        last0 = real & (d0 == qi0)
        gen_cur = (jnp.where(last0, gen_blk + 1, gen_blk), jnp.where(last0, 0, gen_d + i32(real)))
        oth_blk, oth_d = gens[prev]
        cut = dead & (oth_blk == blk1)
        gen_oth = (jnp.where(cut, oth_blk + 1, oth_blk), jnp.where(cut, 0, oth_d))
        new_gens = (gen_cur, gen_oth) if cur == 0 else (gen_oth, gen_cur)
        both_out = (new_gens[0][0] >= n_blk) & (new_gens[1][0] >= n_blk)
        return (new_gens, (blk0, d0, i32(last0)), (blk1, d1, last1 | i32(cut)), stage2,
                jnp.where(both_out, drained + 1, 0))

    def step_pair(state):
        return step(1, step(0, state))

    zero = jnp.int32(0)
    idle = (jnp.int32(n_blk - 1), zero, zero)
    state = (((zero, zero), (zero, zero)), idle, idle, idle, zero)
    lax.while_loop(lambda st: st[-1] < SB_STAGES, step_pair, state)


def _sb_attention(q, k, v):
    b, s, w = q.shape
    assert SB_TQ == SB_TK and s % SB_TQ == 0 and w % (SB_STREAMS * LANES) == 0 and b % SB_BATCH == 0
    assert SB_HEADS_PER_STEP * SB_HEAD_DIM == LANES
    assert s // SB_TQ > 1
    row = lax.broadcasted_iota(jnp.int32, (SB_TQ, SB_TK), 0)
    col = lax.broadcasted_iota(jnp.int32, (SB_TQ, SB_TK), 1)
    tri = (row > col).astype(BF16)
    diag_bias = jnp.where(col < row, 0.0, SB_MASKED).astype(F32)
    bias = jnp.stack([diag_bias, jnp.zeros_like(diag_bias)])
    blk = pl.BlockSpec((SB_BATCH, s, SB_STREAMS * LANES), lambda i, j: (i, 0, j))
    per_stream = lambda cols, dtype: pltpu.VMEM((SB_STREAMS, SB_HEADS_PER_STEP, SB_TQ, cols), dtype)
    per_parity = lambda dtype: pltpu.VMEM((2, SB_HEADS_PER_STEP, SB_TQ, SB_TK), dtype)
    return pl.pallas_call(
        _sb_kernel,
        grid=(b // SB_BATCH, w // (SB_STREAMS * LANES)),
        in_specs=[blk, blk, blk, _resident(tri.shape), _resident(bias.shape)],
        out_specs=blk,
        out_shape=jax.ShapeDtypeStruct((b, s, w), BF16),
        scratch_shapes=[per_parity(F32), per_parity(BF16), per_parity(BF16), per_parity(F32),
                        per_stream(LANES, F32), per_stream(1, F32)],
        compiler_params=_params("parallel", "parallel"),
        name="sb_attn",
    )(q, k, v, tri, bias)


def _pool_mix(u_ref, ubuf, wpool_ref, ps_ref, si):
    ts = u_ref.shape[1]
    gdim = wpool_ref.shape[1]

    @pl.when(si == 0)
    def _():
        ubuf[0:POOL_HALO, :] = jnp.zeros((POOL_HALO, ubuf.shape[1]), F32)

    ubuf[POOL_HALO:, :] = u_ref[0]
    pos = si * ts + lax.broadcasted_iota(jnp.int32, (ts, 1), 0)
    pooled_out = []
    for g, w in enumerate(POOL_WINDOWS):
        cols = slice(g * gdim, (g + 1) * gdim)
        total = ubuf[:, cols]
        shift = 1
        while shift < w:
            total = total + pltpu.roll(total, shift, axis=0)
            shift *= 2
        token = ubuf[POOL_HALO:, cols]
        inv_count = 1.0 / jnp.minimum(pos + 1, w).astype(F32)
        pooled = (total[POOL_HALO:] * inv_count - token).astype(BF16)
        pooled_out.append(jnp.dot(pooled, wpool_ref[g], preferred_element_type=F32))
    o_pool = (jnp.concatenate(pooled_out, axis=1) * ps_ref[...]).astype(BF16)
    ubuf[0:POOL_HALO, :] = ubuf[ts:ts + POOL_HALO, :]
    return o_pool


def _cross_attention(x, g_ref, wq_ref, kv_ref, wo_ref):
    d = x.shape[1]
    hd = d // MEM_HEADS
    h = _rms(x, g_ref[...]).astype(BF16)
    q = (jnp.dot(h, wq_ref[...], preferred_element_type=F32) * hd ** -0.5).astype(BF16)
    heads = []
    for i in range(MEM_HEADS):
        qh = q[:, i * hd:(i + 1) * hd]
        kh = kv_ref[0, :, i * hd:(i + 1) * hd]
        vh = kv_ref[0, :, d + i * hd:d + (i + 1) * hd]
        s = lax.dot_general(qh, kh, (((1,), (1,)), ((), ())), preferred_element_type=F32)
        p = jnp.exp(s - jnp.max(s, axis=-1, keepdims=True))
        inv = 1.0 / jnp.sum(p, axis=-1, keepdims=True)
        heads.append((jnp.dot(p.astype(BF16), vh, preferred_element_type=F32) * inv).astype(BF16))
    o = jnp.concatenate(heads, axis=1)
    return jnp.dot(o, wo_ref[...], preferred_element_type=F32)


def _mix_xattn_kernel(x_ref, osb_ref, u_ref, wout_ref, wpool_ref, ps_ref, g_ref, wq_ref, kv_ref,
                      wo_ref, o_ref, ubuf):
    sb_width = osb_ref.shape[2]
    o_pool = _pool_mix(u_ref, ubuf, wpool_ref, ps_ref, pl.program_id(1))
    y = x_ref[0] + jnp.dot(osb_ref[0], wout_ref[0:sb_width, :], preferred_element_type=F32)
    y = y + jnp.dot(o_pool, wout_ref[sb_width:, :], preferred_element_type=F32)
    o_ref[0] = y + _cross_attention(y, g_ref, wq_ref, kv_ref, wo_ref)


def _mix_xattn(x, o_sb, u, w_out, w_pool, pool_scale, g, w_q, kv, w_o, *, ts=512):
    b, s, d = x.shape
    m = kv.shape[1]
    sb_width = o_sb.shape[2]
    pool_width = u.shape[2]
    assert POOL_HALO >= max(POOL_WINDOWS) - 1 and ts >= POOL_HALO
    blk = lambda n: pl.BlockSpec((1, ts, n), lambda i, j: (i, j, 0))
    return pl.pallas_call(
        _mix_xattn_kernel,
        grid=(b, s // ts),
        in_specs=[blk(d), blk(sb_width), blk(pool_width), _resident(w_out.shape),
                  _resident(w_pool.shape), _resident((1, pool_width)), _resident((1, d)),
                  _resident((d, d)), pl.BlockSpec((1, m, 2 * d), lambda i, j: (i, 0, 0)),
                  _resident((d, d))],
        out_specs=blk(d),
        out_shape=jax.ShapeDtypeStruct((b, s, d), F32),
        scratch_shapes=[pltpu.VMEM((ts + POOL_HALO, pool_width), F32)],
        compiler_params=_params("parallel", "arbitrary"),
        name="mix_xattn",
    )(x, o_sb, u, w_out, w_pool, pool_scale, g, w_q, kv, w_o)


def _memkv_kernel(m_ref, g_ref, w_ref, o_ref):
    h = _rms(m_ref[...], g_ref[...]).astype(BF16)
    o_ref[...] = jnp.dot(h, w_ref[...], preferred_element_type=F32).astype(o_ref.dtype)


def _mem_kv(mem2d, g, w_kv, *, tm=1024):
    t, d = mem2d.shape
    n = w_kv.shape[1]
    return pl.pallas_call(
        _memkv_kernel,
        grid=(t // tm,),
        in_specs=[pl.BlockSpec((tm, d), lambda i: (i, 0)), _resident((1, d)), _resident((d, n))],
        out_specs=pl.BlockSpec((tm, n), lambda i: (i, 0)),
        out_shape=jax.ShapeDtypeStruct((t, n), BF16),
        compiler_params=_params("parallel"),
        name="mem_kv",
    )(mem2d, g, w_kv)


def kernel(x, mem, ffn1_norm, ffn1_w_gate, ffn1_w_up, ffn1_w_down, mix_norm, w_in, w_pool,
           pool_scale, w_out, mem_q_norm, mem_kv_norm, mem_w_q, mem_w_kv, mem_w_o,
           ffn2_norm, ffn2_w_gate, ffn2_w_up, ffn2_w_down, final_norm):
    b, s, d = x.shape
    m = mem.shape[1]
    depth = w_in.shape[0]
    sb_width = w_out.shape[1] // 2
    bf = lambda a: a.astype(BF16)
    vec = lambda a: a.reshape(1, -1)

    x2d = x.reshape(b * s, d)
    for l in range(depth):
        last = l == depth - 1
        x2d = _ffn(x2d, vec(ffn1_norm[l]), bf(ffn1_w_gate[l]), bf(ffn1_w_up[l]), bf(ffn1_w_down[l]))
        q, k, v, u = _proj(x2d, vec(mix_norm[l]), bf(w_in[l]), sb_width=sb_width)
        shp = lambda a: a.reshape(b, s, a.shape[-1])
        o_sb = _sb_attention(shp(q), shp(k), shp(v))
        kv = _mem_kv(mem.reshape(b * m, d), vec(mem_kv_norm[l]), bf(mem_w_kv[l]))
        x3d = _mix_xattn(x2d.reshape(b, s, d), o_sb, shp(u), bf(w_out[l]), bf(w_pool[l]),
                         vec(pool_scale[l]), vec(mem_q_norm[l]), bf(mem_w_q[l]),
                         kv.reshape(b, m, 2 * d), bf(mem_w_o[l]))
        x2d = _ffn(x3d.reshape(b * s, d), vec(ffn2_norm[l]), bf(ffn2_w_gate[l]), bf(ffn2_w_up[l]),
                   bf(ffn2_w_down[l]), vec(final_norm) if last else None)
    return x2d.reshape(b, s, d)
```

```python
import functools

import jax
import jax.numpy as jnp
from jax import lax
from jax.experimental import pallas as pl
from jax.experimental.pallas import tpu as pltpu

F32 = jnp.float32
BF16 = jnp.bfloat16

EPS = 1e-6
FFN_RESIDUAL_WEIGHT = 0.5
FFN_CHUNK = 1536
SB_HEAD_DIM = 64
SB_HEADS_PER_STEP = 2
POOL_WINDOWS = (2, 4, 8, 16)
POOL_HALO = 16
MEM_HEADS = 4

LANES = 128
SUBLANES = 8
SB_TQ = 256
SB_TK = 256
SB_STAGES = 4
SB_STREAMS = 2
SB_BATCH = 4
VMEM_LIMIT = 60 * 1024 * 1024
LOG2_E = 1.4426950408889634
SB_MASKED = -1e30
SB_DEAD_CARRY = 150.0


def _rms(x, g):
    return x * lax.rsqrt(jnp.mean(x * x, axis=-1, keepdims=True) + EPS) * g


def _resident(shape):
    zeros = (0,) * len(shape)
    return pl.BlockSpec(shape, lambda *_: zeros, pipeline_mode=pl.Buffered(1))


def _params(*sem):
    return pltpu.CompilerParams(dimension_semantics=sem, vmem_limit_bytes=VMEM_LIMIT)


def _ffn_kernel(x_ref, g_ref, wg_ref, wu_ref, wd_ref, *rest, final_norm):
    o_ref = rest[-1]
    x = x_ref[...]
    h = _rms(x, g_ref[...]).astype(BF16)
    dff = wg_ref.shape[1]
    out = None
    for c0 in range(0, dff, FFN_CHUNK):
        c1 = min(c0 + FFN_CHUNK, dff)
        gate = jnp.dot(h, wg_ref[:, c0:c1], preferred_element_type=F32)
        up = jnp.dot(h, wu_ref[:, c0:c1], preferred_element_type=F32)
        act = (jax.nn.silu(gate) * up).astype(BF16)
        part = jnp.dot(act, wd_ref[c0:c1, :], preferred_element_type=F32)
        out = part if out is None else out + part
    y = x + FFN_RESIDUAL_WEIGHT * out
    if final_norm:
        y = _rms(y, rest[0][...])
    o_ref[...] = y


def _ffn(x2d, g, wg, wu, wd, gf=None, *, tm=1024):
    t, d = x2d.shape
    dff = wg.shape[1]
    row = pl.BlockSpec((tm, d), lambda i: (i, 0))
    vec = _resident((1, d))
    in_specs = [row, vec, _resident((d, dff)), _resident((d, dff)), _resident((dff, d))]
    args = [x2d, g, wg, wu, wd]
    if gf is not None:
        in_specs.append(vec)
        args.append(gf)
    return pl.pallas_call(
        functools.partial(_ffn_kernel, final_norm=gf is not None),
        grid=(t // tm,),
        in_specs=in_specs,
        out_specs=row,
        out_shape=jax.ShapeDtypeStruct((t, d), F32),
        compiler_params=_params("parallel"),
        name="ffn_final" if gf is not None else "ffn",
    )(*args)


def _proj_kernel(x_ref, g_ref, w_ref, q_ref, k_ref, v_ref, u_ref, *, q_scale):
    h = _rms(x_ref[...], g_ref[...]).astype(BF16)
    p = jnp.dot(h, w_ref[...], preferred_element_type=F32)
    w = q_ref.shape[1]
    q_ref[...] = (p[:, :w] * q_scale).astype(BF16)
    k_ref[...] = p[:, w:2 * w].astype(BF16)
    v_ref[...] = p[:, 2 * w:3 * w].astype(BF16)
    u_ref[...] = p[:, 3 * w:]


def _proj(x2d, g, w_in, *, sb_width, tm=1024):
    t, d = x2d.shape
    cols = w_in.shape[1]
    pool_width = cols - 3 * sb_width
    row = lambda n: pl.BlockSpec((tm, n), lambda i: (i, 0))
    return pl.pallas_call(
        functools.partial(_proj_kernel, q_scale=LOG2_E * SB_HEAD_DIM ** -0.5),
        grid=(t // tm,),
        in_specs=[row(d), _resident((1, d)), _resident((d, cols))],
        out_specs=[row(sb_width), row(sb_width), row(sb_width), row(pool_width)],
        out_shape=[jax.ShapeDtypeStruct((t, sb_width), BF16)] * 3
        + [jax.ShapeDtypeStruct((t, pool_width), F32)],
        compiler_params=_params("parallel"),
        name="proj",
    )(x2d, g, w_in)


def _sb_scores(z2, carry):
    pos = jnp.maximum(z2, 0.0)
    neg = z2 - pos
    soft = jnp.log2(1.0 + jnp.exp2(neg - pos))
    rest = pos + soft
    return rest.astype(BF16), neg - soft - carry, carry + jnp.sum(rest, axis=1, keepdims=True)


def _sb_kernel(q_ref, k_ref, v_ref, tri_ref, bias_ref, o_ref,
               z_ref, rest_ref, wgt_ref, lbc_ref, acc_ref, carry_ref):
    n_q = q_ref.shape[1] // SB_TQ
    n_blk = q_ref.shape[0] * n_q
    heads = range(SB_HEADS_PER_STEP)
    lane = lax.broadcasted_iota(jnp.int32, (1, LANES), 1)
    head_lanes = [lane < SB_HEAD_DIM, lane >= SB_HEAD_DIM]
    nt = (((1,), (1,)), ((), ()))
    i32 = lambda b: b.astype(jnp.int32)

    for ref in (z_ref, rest_ref, wgt_ref, lbc_ref):
        ref[1] = jnp.zeros(ref.shape[1:], ref.dtype)
    for ref in (acc_ref, carry_ref):
        ref[...] = jnp.zeros(ref.shape, ref.dtype)

    def rows(block):
        return pl.ds(pl.multiple_of(block * SB_TK, SB_TK), SB_TK)

    def lanes(stream):
        return slice(stream * LANES, (stream + 1) * LANES)

    def step(cur, state):
        gens, (blk1, d1, last1), stage2, (blk3, d3, last3), drained = state
        prev = 1 - cur
        gen_blk, gen_d = gens[cur]
        real = gen_blk < n_blk
        blk0 = jnp.where(real, gen_blk, n_blk - 1)
        d0 = jnp.where(real, gen_d, 0)
        b0, qi0 = blk0 // n_q, blk0 % n_q
        b3, qi3 = blk3 // n_q, blk3 % n_q
        q = q_ref[b0, rows(qi0), lanes(cur)]
        kb = k_ref[b0, rows(qi0 - d0), lanes(cur)]
        vb = v_ref[b3, rows(qi3 - d3), lanes(prev)]
        bias = bias_ref[jnp.minimum(d0, 1)]
        z_new = [lax.dot_general(jnp.where(head_lanes[h], q, jnp.zeros_like(q)), kb, nt,
                                 preferred_element_type=F32) + bias for h in heads]
        sufs = [jnp.dot(rest_ref[prev, h], tri_ref[...], preferred_element_type=F32) for h in heads]
        pvs = [jnp.dot(wgt_ref[prev, h], vb, preferred_element_type=F32) for h in heads]
        carries = []
        for h in heads:
            carry = jnp.where(d1 == 0, 0.0, carry_ref[prev, h])
            rest_ref[cur, h], lbc_ref[cur, h], carry = _sb_scores(z_ref[prev, h], carry)
            carry_ref[prev, h] = carry
            carries.append(carry)
        for h in heads:
            z_ref[cur, h] = z_new[h]
            wgt_ref[cur, h] = jnp.exp2(lbc_ref[prev, h] - sufs[h]).astype(BF16)
            acc_ref[prev, h] = jnp.where(d3 == 0, 0.0, acc_ref[prev, h]) + pvs[h]

        low = jnp.min(jnp.minimum(*carries).reshape(SB_TQ // SUBLANES, SUBLANES, 1), axis=0)
        for shift in (4, 2, 1):
            low = jnp.minimum(low, pltpu.roll(low, shift, axis=0))
        dead = low[0, 0] >= SB_DEAD_CARRY

        @pl.when(last3 == 1)
        def _():
            o_ref[b3, rows(qi3), lanes(prev)] = jnp.where(
                head_lanes[0], acc_ref[prev, 0], acc_ref[prev, 1]).astype(o_ref.dtype)

        last0 = real & (d0 == qi0)
        gen_cur = (jnp.where(last0, gen_blk + 1, gen_blk), jnp.where(last0, 0, gen_d + i32(real)))
        oth_blk, oth_d = gens[prev]
        cut = dead & (oth_blk == blk1)
        gen_oth = (jnp.where(cut, oth_blk + 1, oth_blk), jnp.where(cut, 0, oth_d))
        new_gens = (gen_cur, gen_oth) if cur == 0 else (gen_oth, gen_cur)
        both_out = (new_gens[0][0] >= n_blk) & (new_gens[1][0] >= n_blk)
        return (new_gens, (blk0, d0, i32(last0)), (blk1, d1, last1 | i32(cut)), stage2,
                jnp.where(both_out, drained + 1, 0))

    def step_pair(state):
        return step(1, step(0, state))

    zero = jnp.int32(0)
    idle = (jnp.int32(n_blk - 1), zero, zero)
    state = (((zero, zero), (zero, zero)), idle, idle, idle, zero)
    lax.while_loop(lambda st: st[-1] < SB_STAGES, step_pair, state)


def _sb_attention(q, k, v):
    b, s, w = q.shape
    assert SB_TQ == SB_TK and s % SB_TQ == 0 and w % (SB_STREAMS * LANES) == 0 and b % SB_BATCH == 0
    assert SB_HEADS_PER_STEP * SB_HEAD_DIM == LANES
    assert s // SB_TQ > 1
    row = lax.broadcasted_iota(jnp.int32, (SB_TQ, SB_TK), 0)
    col = lax.broadcasted_iota(jnp.int32, (SB_TQ, SB_TK), 1)
    tri = (row > col).astype(BF16)
    diag_bias = jnp.where(col < row, 0.0, SB_MASKED).astype(F32)
    bias = jnp.stack([diag_bias, jnp.zeros_like(diag_bias)])
    blk = pl.BlockSpec((SB_BATCH, s, SB_STREAMS * LANES), lambda i, j: (i, 0, j))
    per_stream = lambda cols, dtype: pltpu.VMEM((SB_STREAMS, SB_HEADS_PER_STEP, SB_TQ, cols), dtype)
    per_parity = lambda dtype: pltpu.VMEM((2, SB_HEADS_PER_STEP, SB_TQ, SB_TK), dtype)
    return pl.pallas_call(
        _sb_kernel,
        grid=(b // SB_BATCH, w // (SB_STREAMS * LANES)),
        in_specs=[blk, blk, blk, _resident(tri.shape), _resident(bias.shape)],
        out_specs=blk,
        out_shape=jax.ShapeDtypeStruct((b, s, w), BF16),
        scratch_shapes=[per_parity(F32), per_parity(BF16), per_parity(BF16), per_parity(F32),
                        per_stream(LANES, F32), per_stream(1, F32)],
        compiler_params=_params("parallel", "parallel"),
        name="sb_attn",
    )(q, k, v, tri, bias)


def _pool_mix(u_ref, ubuf, wpool_ref, ps_ref, si):
    ts = u_ref.shape[1]
    gdim = wpool_ref.shape[1]

    @pl.when(si == 0)
    def _():
        ubuf[0:POOL_HALO, :] = jnp.zeros((POOL_HALO, ubuf.shape[1]), F32)

    ubuf[POOL_HALO:, :] = u_ref[0]
    pos = si * ts + lax.broadcasted_iota(jnp.int32, (ts, 1), 0)
    pooled_out = []
    for g, w in enumerate(POOL_WINDOWS):
        cols = slice(g * gdim, (g + 1) * gdim)
        total = ubuf[:, cols]
        shift = 1
        while shift < w:
            total = total + pltpu.roll(total, shift, axis=0)
            shift *= 2
        token = ubuf[POOL_HALO:, cols]
        inv_count = 1.0 / jnp.minimum(pos + 1, w).astype(F32)
        pooled = (total[POOL_HALO:] * inv_count - token).astype(BF16)
        pooled_out.append(jnp.dot(pooled, wpool_ref[g], preferred_element_type=F32))
    o_pool = (jnp.concatenate(pooled_out, axis=1) * ps_ref[...]).astype(BF16)
    ubuf[0:POOL_HALO, :] = ubuf[ts:ts + POOL_HALO, :]
    return o_pool


def _cross_attention(x, g_ref, wq_ref, kv_ref, wo_ref):
    d = x.shape[1]
    hd = d // MEM_HEADS
    h = _rms(x, g_ref[...]).astype(BF16)
    q = (jnp.dot(h, wq_ref[...], preferred_element_type=F32) * hd ** -0.5).astype(BF16)
    heads = []
    for i in range(MEM_HEADS):
        qh = q[:, i * hd:(i + 1) * hd]
        kh = kv_ref[0, :, i * hd:(i + 1) * hd]
        vh = kv_ref[0, :, d + i * hd:d + (i + 1) * hd]
        s = lax.dot_general(qh, kh, (((1,), (1,)), ((), ())), preferred_element_type=F32)
        p = jnp.exp(s - jnp.max(s, axis=-1, keepdims=True))
        inv = 1.0 / jnp.sum(p, axis=-1, keepdims=True)
        heads.append((jnp.dot(p.astype(BF16), vh, preferred_element_type=F32) * inv).astype(BF16))
    o = jnp.concatenate(heads, axis=1)
    return jnp.dot(o, wo_ref[...], preferred_element_type=F32)


def _mix_xattn_kernel(x_ref, osb_ref, u_ref, wout_ref, wpool_ref, ps_ref, g_ref, wq_ref, kv_ref,
                      wo_ref, o_ref, ubuf):
    sb_width = osb_ref.shape[2]
    o_pool = _pool_mix(u_ref, ubuf, wpool_ref, ps_ref, pl.program_id(1))
    y = x_ref[0] + jnp.dot(osb_ref[0], wout_ref[0:sb_width, :], preferred_element_type=F32)
    y = y + jnp.dot(o_pool, wout_ref[sb_width:, :], preferred_element_type=F32)
    o_ref[0] = y + _cross_attention(y, g_ref, wq_ref, kv_ref, wo_ref)


def _mix_xattn(x, o_sb, u, w_out, w_pool, pool_scale, g, w_q, kv, w_o, *, ts=1024):
    b, s, d = x.shape
    m = kv.shape[1]
    sb_width = o_sb.shape[2]
    pool_width = u.shape[2]
    assert POOL_HALO >= max(POOL_WINDOWS) - 1 and ts >= POOL_HALO
    blk = lambda n: pl.BlockSpec((1, ts, n), lambda i, j: (i, j, 0))
    return pl.pallas_call(
        _mix_xattn_kernel,
        grid=(b, s // ts),
        in_specs=[blk(d), blk(sb_width), blk(pool_width), _resident(w_out.shape),
                  _resident(w_pool.shape), _resident((1, pool_width)), _resident((1, d)),
                  _resident((d, d)), pl.BlockSpec((1, m, 2 * d), lambda i, j: (i, 0, 0)),
                  _resident((d, d))],
        out_specs=blk(d),
        out_shape=jax.ShapeDtypeStruct((b, s, d), F32),
        scratch_shapes=[pltpu.VMEM((ts + POOL_HALO, pool_width), F32)],
        compiler_params=_params("parallel", "arbitrary"),
        name="mix_xattn",
    )(x, o_sb, u, w_out, w_pool, pool_scale, g, w_q, kv, w_o)


def _memkv_kernel(m_ref, g_ref, w_ref, o_ref):
    h = _rms(m_ref[...], g_ref[...]).astype(BF16)
    o_ref[...] = jnp.dot(h, w_ref[...], preferred_element_type=F32).astype(o_ref.dtype)


def _mem_kv(mem2d, g, w_kv, *, tm=1024):
    t, d = mem2d.shape
    n = w_kv.shape[1]
    return pl.pallas_call(
        _memkv_kernel,
        grid=(t // tm,),
        in_specs=[pl.BlockSpec((tm, d), lambda i: (i, 0)), _resident((1, d)), _resident((d, n))],
        out_specs=pl.BlockSpec((tm, n), lambda i: (i, 0)),
        out_shape=jax.ShapeDtypeStruct((t, n), BF16),
        compiler_params=_params("parallel"),
        name="mem_kv",
    )(mem2d, g, w_kv)


def kernel(x, mem, ffn1_norm, ffn1_w_gate, ffn1_w_up, ffn1_w_down, mix_norm, w_in, w_pool,
           pool_scale, w_out, mem_q_norm, mem_kv_norm, mem_w_q, mem_w_kv, mem_w_o,
           ffn2_norm, ffn2_w_gate, ffn2_w_up, ffn2_w_down, final_norm):
    b, s, d = x.shape
    m = mem.shape[1]
    depth = w_in.shape[0]
    sb_width = w_out.shape[1] // 2
    bf = lambda a: a.astype(BF16)
    vec = lambda a: a.reshape(1, -1)

    x2d = x.reshape(b * s, d)
    for l in range(depth):
        last = l == depth - 1
        x2d = _ffn(x2d, vec(ffn1_norm[l]), bf(ffn1_w_gate[l]), bf(ffn1_w_up[l]), bf(ffn1_w_down[l]))
        q, k, v, u = _proj(x2d, vec(mix_norm[l]), bf(w_in[l]), sb_width=sb_width)
        shp = lambda a: a.reshape(b, s, a.shape[-1])
        o_sb = _sb_attention(shp(q), shp(k), shp(v))
        kv = _mem_kv(mem.reshape(b * m, d), vec(mem_kv_norm[l]), bf(mem_w_kv[l]))
        x3d = _mix_xattn(x2d.reshape(b, s, d), o_sb, shp(u), bf(w_out[l]), bf(w_pool[l]),
                         vec(pool_scale[l]), vec(mem_q_norm[l]), bf(mem_w_q[l]),
                         kv.reshape(b, m, 2 * d), bf(mem_w_o[l]))
        x2d = _ffn(x3d.reshape(b * s, d), vec(ffn2_norm[l]), bf(ffn2_w_gate[l]), bf(ffn2_w_up[l]),
                   bf(ffn2_w_down[l]), vec(final_norm) if last else None)
    return x2d.reshape(b, s, d)
```

```python
import functools

import jax
import jax.numpy as jnp
from jax import lax
from jax.experimental import pallas as pl
from jax.experimental.pallas import tpu as pltpu

F32 = jnp.float32
BF16 = jnp.bfloat16

EPS = 1e-6
FFN_RESIDUAL_WEIGHT = 0.5
FFN_CHUNK = 1536
SB_HEAD_DIM = 64
SB_HEADS_PER_STEP = 2
POOL_WINDOWS = (2, 4, 8, 16)
POOL_HALO = 16
MEM_HEADS = 4

LANES = 128
SUBLANES = 8
SB_TQ = 256
SB_TK = 256
SB_STAGES = 4
SB_STREAMS = 2
SB_BATCH = 4
VMEM_LIMIT = 60 * 1024 * 1024
LOG2_E = 1.4426950408889634
SB_MASKED = -1e30
SB_DEAD_CARRY = 150.0


def _rms(x, g):
    return x * lax.rsqrt(jnp.mean(x * x, axis=-1, keepdims=True) + EPS) * g


def _resident(shape):
    zeros = (0,) * len(shape)
    return pl.BlockSpec(shape, lambda *_: zeros, pipeline_mode=pl.Buffered(1))


def _params(*sem):
    return pltpu.CompilerParams(dimension_semantics=sem, vmem_limit_bytes=VMEM_LIMIT)


def _ffn_kernel(x_ref, g_ref, wg_ref, wu_ref, wd_ref, *rest, final_norm):
    o_ref = rest[-1]
    x = x_ref[...]
    h = _rms(x, g_ref[...]).astype(BF16)
    dff = wg_ref.shape[1]
    out = None
    for c0 in range(0, dff, FFN_CHUNK):
        c1 = min(c0 + FFN_CHUNK, dff)
        gate = jnp.dot(h, wg_ref[:, c0:c1], preferred_element_type=F32)
        up = jnp.dot(h, wu_ref[:, c0:c1], preferred_element_type=F32)
        act = (jax.nn.silu(gate) * up).astype(BF16)
        part = jnp.dot(act, wd_ref[c0:c1, :], preferred_element_type=F32)
        out = part if out is None else out + part
    y = x + FFN_RESIDUAL_WEIGHT * out
    if final_norm:
        y = _rms(y, rest[0][...])
    o_ref[...] = y


def _ffn(x2d, g, wg, wu, wd, gf=None, *, tm=1024):
    t, d = x2d.shape
    dff = wg.shape[1]
    row = pl.BlockSpec((tm, d), lambda i: (i, 0))
    vec = _resident((1, d))
    in_specs = [row, vec, _resident((d, dff)), _resident((d, dff)), _resident((dff, d))]
    args = [x2d, g, wg, wu, wd]
    if gf is not None:
        in_specs.append(vec)
        args.append(gf)
    return pl.pallas_call(
        functools.partial(_ffn_kernel, final_norm=gf is not None),
        grid=(t // tm,),
        in_specs=in_specs,
        out_specs=row,
        out_shape=jax.ShapeDtypeStruct((t, d), F32),
        compiler_params=_params("parallel"),
        name="ffn_final" if gf is not None else "ffn",
    )(*args)


def _proj_kernel(x_ref, g_ref, w_ref, q_ref, k_ref, v_ref, u_ref, *, q_scale):
    h = _rms(x_ref[...], g_ref[...]).astype(BF16)
    p = jnp.dot(h, w_ref[...], preferred_element_type=F32)
    w = q_ref.shape[1]
    q_ref[...] = (p[:, :w] * q_scale).astype(BF16)
    k_ref[...] = p[:, w:2 * w].astype(BF16)
    v_ref[...] = p[:, 2 * w:3 * w].astype(BF16)
    u_ref[...] = p[:, 3 * w:]


def _proj(x2d, g, w_in, *, sb_width, tm=1024):
    t, d = x2d.shape
    cols = w_in.shape[1]
    pool_width = cols - 3 * sb_width
    row = lambda n: pl.BlockSpec((tm, n), lambda i: (i, 0))
    return pl.pallas_call(
        functools.partial(_proj_kernel, q_scale=LOG2_E * SB_HEAD_DIM ** -0.5),
        grid=(t // tm,),
        in_specs=[row(d), _resident((1, d)), _resident((d, cols))],
        out_specs=[row(sb_width), row(sb_width), row(sb_width), row(pool_width)],
        out_shape=[jax.ShapeDtypeStruct((t, sb_width), BF16)] * 3
        + [jax.ShapeDtypeStruct((t, pool_width), F32)],
        compiler_params=_params("parallel"),
        name="proj",
    )(x2d, g, w_in)


def _sb_scores(z2, carry):
    pos = jnp.maximum(z2, 0.0)
    neg = z2 - pos
    soft = jnp.log2(1.0 + jnp.exp2(neg - pos))
    rest = pos + soft
    return rest.astype(BF16), neg - soft - carry, carry + jnp.sum(rest, axis=1, keepdims=True)


def _sb_kernel(q_ref, k_ref, v_ref, tri_ref, bias_ref, o_ref,
               z_ref, rest_ref, wgt_ref, lbc_ref, acc_ref, carry_ref):
    n_q = q_ref.shape[1] // SB_TQ
    n_blk = q_ref.shape[0] * n_q
    heads = range(SB_HEADS_PER_STEP)
    lane = lax.broadcasted_iota(jnp.int32, (1, LANES), 1)
    head_lanes = [lane < SB_HEAD_DIM, lane >= SB_HEAD_DIM]
    nt = (((1,), (1,)), ((), ()))
    i32 = lambda b: b.astype(jnp.int32)

    for ref in (z_ref, rest_ref, wgt_ref, lbc_ref):
        ref[1] = jnp.zeros(ref.shape[1:], ref.dtype)
    for ref in (acc_ref, carry_ref):
        ref[...] = jnp.zeros(ref.shape, ref.dtype)

    def rows(block):
        return pl.ds(pl.multiple_of(block * SB_TK, SB_TK), SB_TK)

    def lanes(stream):
        return slice(stream * LANES, (stream + 1) * LANES)

    def step(cur, state):
        gens, (blk1, d1, last1), stage2, (blk3, d3, last3), drained = state
        prev = 1 - cur
        gen_blk, gen_d = gens[cur]
        real = gen_blk < n_blk
        blk0 = jnp.where(real, gen_blk, n_blk - 1)
        d0 = jnp.where(real, gen_d, 0)
        b0, qi0 = blk0 // n_q, blk0 % n_q
        b3, qi3 = blk3 // n_q, blk3 % n_q
        q = q_ref[b0, rows(qi0), lanes(cur)]
        kb = k_ref[b0, rows(qi0 - d0), lanes(cur)]
        vb = v_ref[b3, rows(qi3 - d3), lanes(prev)]
        bias = bias_ref[jnp.minimum(d0, 1)]
        z_new = [lax.dot_general(jnp.where(head_lanes[h], q, jnp.zeros_like(q)), kb, nt,
                                 preferred_element_type=F32) + bias for h in heads]
        sufs = [jnp.dot(rest_ref[prev, h], tri_ref[...], preferred_element_type=F32) for h in heads]
        pvs = [jnp.dot(wgt_ref[prev, h], vb, preferred_element_type=F32) for h in heads]
        carries = []
        for h in heads:
            carry = jnp.where(d1 == 0, 0.0, carry_ref[prev, h])
            rest_ref[cur, h], lbc_ref[cur, h], carry = _sb_scores(z_ref[prev, h], carry)
            carry_ref[prev, h] = carry
            carries.append(carry)
        for h in heads:
            z_ref[cur, h] = z_new[h]
            wgt_ref[cur, h] = jnp.exp2(lbc_ref[prev, h] - sufs[h]).astype(BF16)
            acc_ref[prev, h] = jnp.where(d3 == 0, 0.0, acc_ref[prev, h]) + pvs[h]

        low = jnp.min(jnp.minimum(*carries).reshape(SB_TQ // SUBLANES, SUBLANES, 1), axis=0)
        for shift in (4, 2, 1):
            low = jnp.minimum(low, pltpu.roll(low, shift, axis=0))
        dead = low[0, 0] >= SB_DEAD_CARRY

        @pl.when(last3 == 1)
        def _():
            o_ref[b3, rows(qi3), lanes(prev)] = jnp.where(
                head_lanes[0], acc_ref[prev, 0], acc_ref[prev, 1]).astype(o_ref.dtype)

        last0 = real & (d0 == qi0)
        gen_cur = (jnp.where(last0, gen_blk + 1, gen_blk), jnp.where(last0, 0, gen_d + i32(real)))
        oth_blk, oth_d = gens[prev]
        cut = dead & (oth_blk == blk1)
        gen_oth = (jnp.where(cut, oth_blk + 1, oth_blk), jnp.where(cut, 0, oth_d))
        new_gens = (gen_cur, gen_oth) if cur == 0 else (gen_oth, gen_cur)
        both_out = (new_gens[0][0] >= n_blk) & (new_gens[1][0] >= n_blk)
        return (new_gens, (blk0, d0, i32(last0)), (blk1, d1, last1 | i32(cut)), stage2,
                jnp.where(both_out, drained + 1, 0))

    def step_pair(state):
        return step(1, step(0, state))

    zero = jnp.int32(0)
    idle = (jnp.int32(n_blk - 1), zero, zero)
    state = (((zero, zero), (zero, zero)), idle, idle, idle, zero)
    lax.while_loop(lambda st: st[-1] < SB_STAGES, step_pair, state)


def _sb_attention(q, k, v):
    b, s, w = q.shape
    assert SB_TQ == SB_TK and s % SB_TQ == 0 and w % (SB_STREAMS * LANES) == 0 and b % SB_BATCH == 0
    assert SB_HEADS_PER_STEP * SB_HEAD_DIM == LANES
    assert s // SB_TQ > 1
    row = lax.broadcasted_iota(jnp.int32, (SB_TQ, SB_TK), 0)
    col = lax.broadcasted_iota(jnp.int32, (SB_TQ, SB_TK), 1)
    tri = (row > col).astype(BF16)
    diag_bias = jnp.where(col < row, 0.0, SB_MASKED).astype(F32)
    bias = jnp.stack([diag_bias, jnp.zeros_like(diag_bias)])
    blk = pl.BlockSpec((SB_BATCH, s, SB_STREAMS * LANES), lambda i, j: (i, 0, j))
    per_stream = lambda cols, dtype: pltpu.VMEM((SB_STREAMS, SB_HEADS_PER_STEP, SB_TQ, cols), dtype)
    per_parity = lambda dtype: pltpu.VMEM((2, SB_HEADS_PER_STEP, SB_TQ, SB_TK), dtype)
    return pl.pallas_call(
        _sb_kernel,
        grid=(b // SB_BATCH, w // (SB_STREAMS * LANES)),
        in_specs=[blk, blk, blk, _resident(tri.shape), _resident(bias.shape)],
        out_specs=blk,
        out_shape=jax.ShapeDtypeStruct((b, s, w), BF16),
        scratch_shapes=[per_parity(F32), per_parity(BF16), per_parity(BF16), per_parity(F32),
                        per_stream(LANES, F32), per_stream(1, F32)],
        compiler_params=_params("parallel", "parallel"),
        name="sb_attn",
    )(q, k, v, tri, bias)


def _pool_mix(u_ref, ubuf, wpool_ref, ps_ref, si):
    ts = u_ref.shape[1]
    gdim = wpool_ref.shape[1]

    @pl.when(si == 0)
    def _():
        ubuf[0:POOL_HALO, :] = jnp.zeros((POOL_HALO, ubuf.shape[1]), F32)

    ubuf[POOL_HALO:, :] = u_ref[0]
    pos = si * ts + lax.broadcasted_iota(jnp.int32, (ts, 1), 0)
    pooled_out = []
    for g, w in enumerate(POOL_WINDOWS):
        cols = slice(g * gdim, (g + 1) * gdim)
        total = ubuf[:, cols]
        shift = 1
        while shift < w:
            total = total + pltpu.roll(total, shift, axis=0)
            shift *= 2
        token = ubuf[POOL_HALO:, cols]
        inv_count = 1.0 / jnp.minimum(pos + 1, w).astype(F32)
        pooled = (total[POOL_HALO:] * inv_count - token).astype(BF16)
        pooled_out.append(jnp.dot(pooled, wpool_ref[g], preferred_element_type=F32))
    o_pool = (jnp.concatenate(pooled_out, axis=1) * ps_ref[...]).astype(BF16)
    ubuf[0:POOL_HALO, :] = ubuf[ts:ts + POOL_HALO, :]
    return o_pool


def _cross_attention(x, g_ref, wq_ref, kv_ref, wo_ref):
    d = x.shape[1]
    hd = d // MEM_HEADS
    h = _rms(x, g_ref[...]).astype(BF16)
    q = (jnp.dot(h, wq_ref[...], preferred_element_type=F32) * hd ** -0.5).astype(BF16)
    heads = []
    for i in range(MEM_HEADS):
        qh = q[:, i * hd:(i + 1) * hd]
        kh = kv_ref[:, i * hd:(i + 1) * hd]
        vh = kv_ref[:, d + i * hd:d + (i + 1) * hd]
        s = lax.dot_general(qh, kh, (((1,), (1,)), ((), ())), preferred_element_type=F32)
        p = jnp.exp(s - jnp.max(s, axis=-1, keepdims=True))
        inv = 1.0 / jnp.sum(p, axis=-1, keepdims=True)
        heads.append((jnp.dot(p.astype(BF16), vh, preferred_element_type=F32) * inv).astype(BF16))
    o = jnp.concatenate(heads, axis=1)
    return jnp.dot(o, wo_ref[...], preferred_element_type=F32)


def _mix_xattn_kernel(x_ref, osb_ref, u_ref, wout_ref, wpool_ref, ps_ref, g_ref, wq_ref, mem_ref,
                      gkv_ref, wkv_ref, wo_ref, o_ref, ubuf, kv_ref):
    sb_width = osb_ref.shape[2]

    @pl.when(pl.program_id(1) == 0)
    def _():
        hm = _rms(mem_ref[0], gkv_ref[...]).astype(BF16)
        kv_ref[...] = jnp.dot(hm, wkv_ref[...], preferred_element_type=F32).astype(BF16)

    o_pool = _pool_mix(u_ref, ubuf, wpool_ref, ps_ref, pl.program_id(1))
    y = x_ref[0] + jnp.dot(osb_ref[0], wout_ref[0:sb_width, :], preferred_element_type=F32)
    y = y + jnp.dot(o_pool, wout_ref[sb_width:, :], preferred_element_type=F32)
    o_ref[0] = y + _cross_attention(y, g_ref, wq_ref, kv_ref, wo_ref)


def _mix_xattn(x, o_sb, u, w_out, w_pool, pool_scale, g, w_q, mem, g_kv, w_kv, w_o, *, ts=1024):
    b, s, d = x.shape
    m = mem.shape[1]
    sb_width = o_sb.shape[2]
    pool_width = u.shape[2]
    assert POOL_HALO >= max(POOL_WINDOWS) - 1 and ts >= POOL_HALO
    blk = lambda n: pl.BlockSpec((1, ts, n), lambda i, j: (i, j, 0))
    return pl.pallas_call(
        _mix_xattn_kernel,
        grid=(b, s // ts),
        in_specs=[blk(d), blk(sb_width), blk(pool_width), _resident(w_out.shape),
                  _resident(w_pool.shape), _resident((1, pool_width)), _resident((1, d)),
                  _resident((d, d)), pl.BlockSpec((1, m, d), lambda i, j: (i, 0, 0)),
                  _resident((1, d)), _resident(w_kv.shape), _resident((d, d))],
        out_specs=blk(d),
        out_shape=jax.ShapeDtypeStruct((b, s, d), F32),
        scratch_shapes=[pltpu.VMEM((ts + POOL_HALO, pool_width), F32),
                        pltpu.VMEM((m, w_kv.shape[1]), BF16)],
        compiler_params=_params("parallel", "arbitrary"),
        name="mix_xattn",
    )(x, o_sb, u, w_out, w_pool, pool_scale, g, w_q, mem, g_kv, w_kv, w_o)


def kernel(x, mem, ffn1_norm, ffn1_w_gate, ffn1_w_up, ffn1_w_down, mix_norm, w_in, w_pool,
           pool_scale, w_out, mem_q_norm, mem_kv_norm, mem_w_q, mem_w_kv, mem_w_o,
           ffn2_norm, ffn2_w_gate, ffn2_w_up, ffn2_w_down, final_norm):
    b, s, d = x.shape
    depth = w_in.shape[0]
    sb_width = w_out.shape[1] // 2
    bf = lambda a: a.astype(BF16)
    vec = lambda a: a.reshape(1, -1)

    x2d = x.reshape(b * s, d)
    for l in range(depth):
        last = l == depth - 1
        x2d = _ffn(x2d, vec(ffn1_norm[l]), bf(ffn1_w_gate[l]), bf(ffn1_w_up[l]), bf(ffn1_w_down[l]))
        q, k, v, u = _proj(x2d, vec(mix_norm[l]), bf(w_in[l]), sb_width=sb_width)
        shp = lambda a: a.reshape(b, s, a.shape[-1])
        o_sb = _sb_attention(shp(q), shp(k), shp(v))
        x3d = _mix_xattn(x2d.reshape(b, s, d), o_sb, shp(u), bf(w_out[l]), bf(w_pool[l]),
                         vec(pool_scale[l]), vec(mem_q_norm[l]), bf(mem_w_q[l]),
                         mem, vec(mem_kv_norm[l]), bf(mem_w_kv[l]), bf(mem_w_o[l]))
        x2d = _ffn(x3d.reshape(b * s, d), vec(ffn2_norm[l]), bf(ffn2_w_gate[l]), bf(ffn2_w_up[l]),
                   bf(ffn2_w_down[l]), vec(final_norm) if last else None)
    return x2d.reshape(b, s, d)
```
